```python
import math
import jax, jax.numpy as jnp
from jax import lax
import numpy as np

D_MODEL = 4096
BATCH = 1
SEQ = 8192
DEPTH = 2

A_GROUPS = ((128, 1), (512, 4), (2048, 16))
N_GROUPS_A = 3
HEADS_PER_GROUP = 4
HD_A = 128
H_A = N_GROUPS_A * HEADS_PER_GROUP
W_A = H_A * HD_A
BAND_BLOCK = 128
H_B = 4
HD_B = 384
W_B = H_B * HD_B
CHUNK = 128
CONV_W = 4
MEM_LEN = 256
H_M = 4
HD_M = 256
W_M = H_M * HD_M
N_BRANCH = 3
N_BUCKETS = 32
MAX_DIST = 2048
D_FF = 11008
N_EXPERTS = 8
TOP_K = 2
D_FF_E = 3584
N_DENSE = (DEPTH + 1) // 2
N_MOE = DEPTH // 2
P_IN = 3 * W_A + 3 * W_B + 2 * H_B + W_B + W_M + N_BRANCH * D_MODEL
EPS = 1e-6

kernel_name = 'hybrid_dilated_mlstm_mem_moe'


def rms_norm(x, g):
    xf = x.astype(jnp.float32)
    y = xf * lax.rsqrt(jnp.mean(xf * xf, axis=-1, keepdims=True) + EPS)
    return (y * g.astype(jnp.float32)).astype(x.dtype)


def t5_bucket(dist):
    max_exact = N_BUCKETS // 2
    d = jnp.maximum(dist, 1).astype(jnp.float32)
    large = max_exact + (jnp.log(d / max_exact) / math.log(MAX_DIST / max_exact)
                         * (N_BUCKETS - max_exact)).astype(jnp.int32)
    large = jnp.minimum(large, N_BUCKETS - 1)
    return jnp.where(dist < max_exact, dist, large)


def band_offsets():
    qi = jnp.arange(BAND_BLOCK)[:, None]
    ki = jnp.arange(2 * BAND_BLOCK)[None, :]
    return qi + BAND_BLOCK - ki, ki


def band_bias(table, dil):
    rel, _ = band_offsets()
    b = table[t5_bucket(jnp.maximum(rel, 0) * dil)]
    return jnp.transpose(b, (2, 0, 1)).astype(jnp.float32)


def dilated_band_attention(q, k, v, bias, steps, dil):
    B, S, H, D = q.shape
    Ls = S // dil
    nblk = -(-Ls // BAND_BLOCK)
    Lp = nblk * BAND_BLOCK
    Z = B * dil

    def to_sub(t):
        t = t.reshape(B, Ls, dil, H, D).transpose(0, 2, 1, 3, 4).reshape(Z, Ls, H, D)
        return jnp.pad(t, ((0, 0), (0, Lp - Ls), (0, 0), (0, 0)))

    def blk(u):
        return u.reshape(Z, nblk, BAND_BLOCK, H, D)

    def band(t):
        prev = jnp.pad(t, ((0, 0), (BAND_BLOCK, 0), (0, 0), (0, 0)))[:, :Lp]
        return jnp.concatenate([blk(prev), blk(t)], axis=2)

    qs, ks, vs = to_sub(q), to_sub(k), to_sub(v)
    kw, vw = band(ks), band(vs)
    logits = jnp.einsum('znqhd,znkhd->znhqk', blk(qs), kw,
                        preferred_element_type=jnp.float32) + bias
    rel, ki = band_offsets()
    key_pos = jnp.arange(nblk)[:, None, None] * BAND_BLOCK + ki[None] - BAND_BLOCK
    mask = (rel >= 0) & (rel <= steps) & (key_pos >= 0)
    logits = jnp.where(mask[None, :, None], logits, -jnp.inf)
    lse = jax.nn.logsumexp(logits, axis=-1)
    p = jnp.exp(logits - lse[..., None])
    o = jnp.einsum('znhqk,znkhd->znqhd', p.astype(v.dtype), vw)
    o = o.reshape(B, dil, Lp, H, D)[:, :, :Ls].transpose(0, 2, 1, 3, 4).reshape(B, S, H, D)
    lse = lse.transpose(0, 1, 3, 2).reshape(B, dil, Lp, H)[:, :, :Ls]
    lse = lse.transpose(0, 2, 1, 3).reshape(B, S, H)
    return o, lse


def causal_conv(x, w, b):
    C = x.shape[-1]
    y = lax.conv_general_dilated(x, w[:, None, :].astype(x.dtype), window_strides=(1,),
                                 padding=((CONV_W - 1, 0),),
                                 dimension_numbers=('NWC', 'WIO', 'NWC'),
                                 feature_group_count=C)
    return y + b


def mlstm_chunkwise(q, k, v, i_pre, f_pre):
    B, S, H, D = q.shape
    NC = S // CHUNK

    def chunks(t):
        return t.reshape(B, NC, CHUNK, H, -1).transpose(0, 3, 1, 2, 4)

    qc, kc, vc = chunks(q), chunks(k), chunks(v)
    li = i_pre.reshape(B, NC, CHUNK, H).transpose(0, 3, 1, 2)
    lf = jax.nn.log_sigmoid(f_pre).reshape(B, NC, CHUNK, H).transpose(0, 3, 1, 2)
    F = jnp.cumsum(lf, axis=-1)
    F_last = F[..., -1]
    causal = jnp.tril(jnp.ones((CHUNK, CHUNK), dtype=bool))
    log_w = jnp.where(causal, F[..., :, None] - F[..., None, :] + li[..., None, :], -jnp.inf)
    m_intra = jnp.max(log_w, axis=-1)
    a = F_last[..., None] - F + li
    b = jnp.max(a, axis=-1)
    ea = jnp.exp(a - b[..., None])
    kv_c = jnp.einsum('bhcld,bhcle->bhcde', ea[..., None] * kc, vc)
    n_c = jnp.einsum('bhcl,bhcld->bhcd', ea, kc)

    def step(carry, xs):
        C, n, m = carry
        fl, bc, kv, nn = xs
        m_new = jnp.maximum(fl + m, bc)
        decay = jnp.exp(fl + m - m_new)
        inj = jnp.exp(bc - m_new)
        C_new = decay[..., None, None] * C + inj[..., None, None] * kv
        n_new = decay[..., None] * n + inj[..., None] * nn
        return (C_new, n_new, m_new), (C, n, m)

    init = (jnp.zeros((B, H, D, D), jnp.float32), jnp.zeros((B, H, D), jnp.float32),
            jnp.zeros((B, H), jnp.float32))
    xs = (jnp.moveaxis(F_last, 2, 0), jnp.moveaxis(b, 2, 0),
          jnp.moveaxis(kv_c, 2, 0), jnp.moveaxis(n_c, 2, 0))
    _, (C_prev, n_prev, m_prev) = lax.scan(step, init, xs)
    C_prev = jnp.moveaxis(C_prev, 0, 2)
    n_prev = jnp.moveaxis(n_prev, 0, 2)
    m_prev = jnp.moveaxis(m_prev, 0, 2)
    m_inter = F + m_prev[..., None]
    m_t = jnp.maximum(m_inter, m_intra)
    s = jnp.einsum('bhctd,bhcsd->bhcts', qc, kc) * jnp.exp(log_w - m_t[..., None])
    inter = jnp.exp(m_inter - m_t)
    num = (jnp.einsum('bhcts,bhcse->bhcte', s, vc)
           + inter[..., None] * jnp.einsum('bhctd,bhcde->bhcte', qc, C_prev))
    den = jnp.sum(s, axis=-1) + inter * jnp.einsum('bhctd,bhcd->bhct', qc, n_prev)
    h = num / jnp.maximum(jnp.abs(den), jnp.exp(-m_t))[..., None]
    return h.transpose(0, 2, 3, 1, 4).reshape(B, S, H, D)


def hybrid_mixer(h, mem, rel_bias, norm_mem, w_in, qn_a, kn_a, conv_w, conv_b, gate_bias_b,
                 hnorm_b, w_mem_kv, qn_m, kn_m, w_br_a, w_br_b, w_br_m, w_out):
    B, S, _ = h.shape
    f32 = jnp.float32
    sizes = (W_A, W_A, W_A, W_B, W_B, W_B, H_B, H_B, W_B, W_M) + (D_MODEL,) * N_BRANCH
    cuts, c = [], 0
    for sz in sizes[:-1]:
        c += sz
        cuts.append(c)
    (qa, ka, va, qb, kb, vb, ib, fb, ob, qm, g_a, g_b, g_m) = jnp.split(h @ w_in, cuts, axis=-1)

    qa = rms_norm(qa.reshape(B, S, H_A, HD_A), qn_a) * (HD_A ** -0.5)
    ka = rms_norm(ka.reshape(B, S, H_A, HD_A), kn_a)
    va = va.reshape(B, S, H_A, HD_A)
    outs, lses = [], []
    for g, (win, dil) in enumerate(A_GROUPS):
        hs = slice(g * HEADS_PER_GROUP, (g + 1) * HEADS_PER_GROUP)
        o, l = dilated_band_attention(qa[:, :, hs], ka[:, :, hs], va[:, :, hs],
                                      band_bias(rel_bias[:, hs], dil), win // dil, dil)
        outs.append(o)
        lses.append(l)
    alpha = jax.nn.softmax(jnp.stack(lses), axis=0)
    y_a = jnp.concatenate([(alpha[g][..., None] * outs[g]).astype(h.dtype)
                           for g in range(N_GROUPS_A)], axis=2).reshape(B, S, W_A)

    qk = jax.nn.silu(causal_conv(jnp.concatenate([qb, kb], axis=-1), conv_w, conv_b))
    qb, kb = jnp.split(qk, 2, axis=-1)
    gbias = gate_bias_b.astype(f32)
    hb = mlstm_chunkwise(qb.reshape(B, S, H_B, HD_B).astype(f32),
                         (kb.reshape(B, S, H_B, HD_B) * (HD_B ** -0.5)).astype(f32),
                         vb.reshape(B, S, H_B, HD_B).astype(f32),
                         ib.astype(f32) + gbias[:H_B], fb.astype(f32) + gbias[H_B:])
    hb = rms_norm(hb, hnorm_b.reshape(H_B, HD_B)).astype(h.dtype)
    y_b = (jax.nn.sigmoid(ob.reshape(B, S, H_B, HD_B)) * hb).reshape(B, S, W_B)

    M = mem.shape[1]
    km, vm = jnp.split(rms_norm(mem, norm_mem) @ w_mem_kv, 2, axis=-1)
    km = rms_norm(km.reshape(B, M, H_M, HD_M), kn_m)
    vm = vm.reshape(B, M, H_M, HD_M)
    qm = rms_norm(qm.reshape(B, S, H_M, HD_M), qn_m) * (HD_M ** -0.5)
    pm = jax.nn.softmax(jnp.einsum('bshd,bmhd->bhsm', qm, km, preferred_element_type=f32), axis=-1)
    y_m = jnp.einsum('bhsm,bmhd->bshd', pm.astype(vm.dtype), vm).reshape(B, S, W_M)

    y = (jax.nn.sigmoid(g_a) * (y_a @ w_br_a) + jax.nn.sigmoid(g_b) * (y_b @ w_br_b)
         + jax.nn.sigmoid(g_m) * (y_m @ w_br_m))
    return y @ w_out


def swiglu(h, w_gate, w_up, w_down):
    return (jax.nn.silu(h @ w_gate) * (h @ w_up)) @ w_down


def moe_swiglu(h, w_router, b_router, w_gate, w_up, w_down):
    logits = jnp.einsum('bsd,de->bse', h, w_router,
                        preferred_element_type=jnp.float32) + b_router.astype(jnp.float32)
    top_v, top_i = lax.top_k(logits, TOP_K)
    gates = jax.nn.softmax(top_v, axis=-1)
    combine = jnp.sum(jax.nn.one_hot(top_i, N_EXPERTS, dtype=jnp.float32) * gates[..., None],
                      axis=-2).astype(h.dtype)
    y = jnp.zeros_like(h)
    for e in range(N_EXPERTS):
        y = y + combine[..., e:e + 1] * swiglu(h, w_gate[e], w_up[e], w_down[e])
    return y


def setup_inputs(seed: int = 0) -> dict:
    key = jax.random.key(seed)
    ks = iter(jax.random.split(key, 40))

    def nrm(shape, scale):
        return jax.random.normal(next(ks), shape, jnp.float32) * scale

    def gain(shape):
        return 1.0 + 0.1 * jax.random.normal(next(ks), shape, jnp.float32)

    L = DEPTH
    return {
        'x': nrm((BATCH, SEQ, D_MODEL), 1.0),
        'mem': nrm((BATCH, MEM_LEN, D_MODEL), 1.0),
        'rel_bias': nrm((N_BUCKETS, H_A), 0.5),
        'norm_mix': gain((L, D_MODEL)),
        'norm_ffn': gain((L, D_MODEL)),
        'norm_mem': gain((L, D_MODEL)),
        'w_in': nrm((L, D_MODEL, P_IN), D_MODEL ** -0.5),
        'qn_a': gain((L, HD_A)),
        'kn_a': gain((L, HD_A)),
        'conv_w': nrm((L, CONV_W, 2 * W_B), CONV_W ** -0.5),
        'conv_b': nrm((L, 2 * W_B), 0.01),
        'gate_bias_b': jnp.concatenate([nrm((L, H_B), 0.1), 3.0 + nrm((L, H_B), 0.1)], axis=-1),
        'hnorm_b': gain((L, W_B)),
        'w_mem_kv': nrm((L, D_MODEL, 2 * W_M), D_MODEL ** -0.5),
        'qn_m': gain((L, HD_M)),
        'kn_m': gain((L, HD_M)),
        'w_br_a': nrm((L, W_A, D_MODEL), W_A ** -0.5),
        'w_br_b': nrm((L, W_B, D_MODEL), W_B ** -0.5),
        'w_br_m': nrm((L, W_M, D_MODEL), W_M ** -0.5),
        'w_out': nrm((L, D_MODEL, D_MODEL), D_MODEL ** -0.5),
        'w_ff_gate': nrm((N_DENSE, D_MODEL, D_FF), D_MODEL ** -0.5),
        'w_ff_up': nrm((N_DENSE, D_MODEL, D_FF), D_MODEL ** -0.5),
        'w_ff_down': nrm((N_DENSE, D_FF, D_MODEL), D_FF ** -0.5),
        'w_router': nrm((N_MOE, D_MODEL, N_EXPERTS), D_MODEL ** -0.5),
        'b_router': nrm((N_MOE, N_EXPERTS), 0.01),
        'w_e_gate': nrm((N_MOE, N_EXPERTS, D_MODEL, D_FF_E), D_MODEL ** -0.5),
        'w_e_up': nrm((N_MOE, N_EXPERTS, D_MODEL, D_FF_E), D_MODEL ** -0.5),
        'w_e_down': nrm((N_MOE, N_EXPERTS, D_FF_E, D_MODEL), D_FF_E ** -0.5),
    }


def reference(x, mem, rel_bias, norm_mix, norm_ffn, norm_mem, w_in, qn_a, kn_a, conv_w, conv_b,
              gate_bias_b, hnorm_b, w_mem_kv, qn_m, kn_m, w_br_a, w_br_b, w_br_m, w_out,
              w_ff_gate, w_ff_up, w_ff_down, w_router, b_router, w_e_gate, w_e_up, w_e_down):
    for l in range(DEPTH):
        h = rms_norm(x, norm_mix[l])
        x = x + hybrid_mixer(h, mem, rel_bias, norm_mem[l], w_in[l], qn_a[l], kn_a[l], conv_w[l],
                             conv_b[l], gate_bias_b[l], hnorm_b[l], w_mem_kv[l], qn_m[l], kn_m[l],
                             w_br_a[l], w_br_b[l], w_br_m[l], w_out[l])
        h = rms_norm(x, norm_ffn[l])
        if l % 2 == 0:
            x = x + swiglu(h, w_ff_gate[l // 2], w_ff_up[l // 2], w_ff_down[l // 2])
        else:
            x = x + moe_swiglu(h, w_router[l // 2], b_router[l // 2], w_e_gate[l // 2],
                               w_e_up[l // 2], w_e_down[l // 2])
    return x
```

```python
import functools
from typing import NamedTuple

import numpy as np
import jax
import jax.numpy as jnp
from jax import lax
from jax.experimental import pallas as pl
from jax.experimental.pallas import tpu as pltpu

F32 = jnp.float32
BF16 = jnp.bfloat16
I32 = jnp.int32
EPS = 1e-6
NEG = -1e30
MIB = 1 << 20
LANES = 128
SUBLANES = 8
VMEM_CAP_MIB = 56


class Cfg(NamedTuple):
    a_groups: tuple = ((128, 1), (512, 4), (2048, 16))
    heads_per_group: int = 4
    hd_a: int = 128
    band_block: int = 128
    h_b: int = 4
    hd_b: int = 384
    chunk: int = 128
    conv_w: int = 4
    h_m: int = 4
    hd_m: int = 256
    n_buckets: int = 32
    max_dist: int = 2048
    n_experts: int = 8
    tm: int = 1024
    tn: int = 512
    tn2: int = 256
    tg: int = 512
    tk_down: int = 2816


def _cparams(n_axes, vmem_mib):
    return pltpu.CompilerParams(dimension_semantics=("arbitrary",) * n_axes,
                                vmem_limit_bytes=int(min(vmem_mib, VMEM_CAP_MIB)) * MIB)


def _round_up(a, b):
    return -(-a // b) * b


def _sigmoid(x):
    return 1.0 / (1.0 + jnp.exp(-x))


def _rms(x, g):
    return x * lax.rsqrt(jnp.mean(x * x, axis=-1, keepdims=True) + EPS) * g


def _rmsnorm_kernel(x_ref, g_ref, o_ref):
    o_ref[...] = _rms(x_ref[...].astype(F32), g_ref[...]).astype(o_ref.dtype)


def rmsnorm(x, g3, l, out_dtype=BF16, tm=256):
    M, D = x.shape
    tm = min(tm, M)
    return pl.pallas_call(
        _rmsnorm_kernel,
        grid=(M // tm,),
        in_specs=[pl.BlockSpec((tm, D), lambda i: (i, 0)),
                  pl.BlockSpec((None, 1, D), lambda i: (l, 0, 0))],
        out_specs=pl.BlockSpec((tm, D), lambda i: (i, 0)),
        out_shape=jax.ShapeDtypeStruct((M, D), out_dtype),
        compiler_params=_cparams(1, 32),
        name="rmsnorm",
    )(x, g3)


def _mm_kernel(*refs, nw, cast, epi, n_valid, tn):
    lhs_ref = refs[0]
    w_refs = refs[1:1 + nw]
    pos = 1 + nw
    res_ref = None
    if epi == "residual":
        res_ref = refs[pos]
        pos += 1
    out_ref = refs[pos]
    pos += 1
    ws_refs = refs[pos:pos + nw] if cast else w_refs

    if cast:
        @pl.when(pl.program_id(1) == 0)
        def _():
            for w_ref, ws_ref in zip(w_refs, ws_refs):
                ws_ref[...] = w_ref[...].astype(BF16)

    lhs = lhs_ref[...]
    accs = [jnp.dot(lhs, ws_ref[...], preferred_element_type=F32) for ws_ref in ws_refs]
    if epi == "plain":
        out = accs[0]
    elif epi == "residual":
        out = res_ref[...] + accs[0]
    else:
        g, u = accs
        out = g * _sigmoid(g) * u
        if n_valid is not None:
            col = pl.program_id(0) * tn + lax.broadcasted_iota(I32, out.shape, 1)
            out = jnp.where(col < n_valid, out, 0.0)
    out_ref[...] = out.astype(out_ref.dtype)


def matmul(lhs, ws, *, lead=None, col0=0, n_out, epi="plain", res=None, n_valid=None,
           out_dtype=BF16, tm=1024, tn=512, name="matmul"):
    M, K = lhs.shape
    nw = len(ws)
    cast = ws[0].dtype != BF16
    tm = min(tm, M)
    while col0 % tn:
        tn //= 2
    assert tn >= LANES and M % tm == 0
    off = col0 // tn
    gn = pl.cdiv(n_out, tn)
    wlast = pl.cdiv(ws[0].shape[-1], tn) - 1
    if lead is None:
        w_spec = pl.BlockSpec((K, tn), lambda n, m: (0, jnp.minimum(n + off, wlast)))
    else:
        w_spec = pl.BlockSpec((None, K, tn), lambda n, m: (lead, 0, jnp.minimum(n + off, wlast)))
    in_specs = [pl.BlockSpec((tm, K), lambda n, m: (m, 0))] + [w_spec] * nw
    args = [lhs] + list(ws)
    if epi == "residual":
        in_specs.append(pl.BlockSpec((tm, tn), lambda n, m: (m, n)))
        args.append(res)
    wbytes = ws[0].dtype.itemsize
    obytes = jnp.dtype(out_dtype).itemsize
    vmem = (2 * tm * K * 2 + nw * 2 * K * tn * wbytes + (nw * K * tn * 2 if cast else 0)
            + 2 * tm * tn * obytes + (2 * tm * tn * 4 if epi == "residual" else 0)
            + (nw + 1) * tm * tn * 4)
    return pl.pallas_call(
        functools.partial(_mm_kernel, nw=nw, cast=cast, epi=epi, n_valid=n_valid, tn=tn),
        grid=(gn, M // tm),
        in_specs=in_specs,
        out_specs=pl.BlockSpec((tm, tn), lambda n, m: (m, n)),
        out_shape=jax.ShapeDtypeStruct((M, n_out), out_dtype),
        scratch_shapes=[pltpu.VMEM((K, tn), BF16)] * (nw if cast else 0),
        compiler_params=_cparams(2, vmem // MIB + 6),
        name=name,
    )(*args)


def _mmk_kernel(lhs_ref, w_ref, res_ref, out_ref, acc_ref):
    k = pl.program_id(2)

    @pl.when(k == 0)
    def _():
        acc_ref[...] = jnp.zeros_like(acc_ref)

    acc_ref[...] += jnp.dot(lhs_ref[...], w_ref[...], preferred_element_type=F32)

    @pl.when(k == pl.num_programs(2) - 1)
    def _():
        out_ref[...] = res_ref[...] + acc_ref[...]


def matmul_ktiled_residual(lhs, w, res, *, tm=1024, tn=1024, tk=1024, name="matmul_k"):
    M, K = lhs.shape
    N = w.shape[1]
    tm, tn, tk = min(tm, M), min(tn, N), min(tk, K)
    assert M % tm == 0 and N % tn == 0 and K % tk == 0
    vmem = 2 * tm * tk * 2 + 2 * tk * tn * 2 + 5 * tm * tn * 4
    return pl.pallas_call(
        _mmk_kernel,
        grid=(M // tm, N // tn, K // tk),
        in_specs=[pl.BlockSpec((tm, tk), lambda m, n, k: (m, k)),
                  pl.BlockSpec((tk, tn), lambda m, n, k: (k, n)),
                  pl.BlockSpec((tm, tn), lambda m, n, k: (m, n))],
        out_specs=pl.BlockSpec((tm, tn), lambda m, n, k: (m, n)),
        out_shape=jax.ShapeDtypeStruct((M, N), F32),
        scratch_shapes=[pltpu.VMEM((tm, tn), F32)],
        compiler_params=_cparams(3, vmem // MIB + 6),
        name=name,
    )(lhs, w, res)


def _t5_bucket_np(dist, n_buckets, max_dist):
    max_exact = n_buckets // 2
    d = np.maximum(dist, 1).astype(np.float32)
    large = max_exact + (np.log(d / np.float32(max_exact)) / np.float32(np.log(max_dist / max_exact))
                         * np.float32(n_buckets - max_exact)).astype(np.int32)
    large = np.minimum(large, n_buckets - 1)
    return np.where(dist < max_exact, dist, large).astype(np.int32)


def _band_kernel(tab_ref, bkt_ref, q_ref, kp_ref, kc_ref, vp_ref, vc_ref, qn_ref, kn_ref,
                 o_ref, lse_ref, bias_ref, *, hpg, hd, head0, buckets, scale):
    p = pl.program_id(0)
    n = pl.program_id(1)
    bb = q_ref.shape[0]

    @pl.when((p == 0) & (n == 0))
    def _():
        bkt = bkt_ref[...]
        for h in range(hpg):
            acc = jnp.full((bb, 2 * bb), NEG, F32)
            for b in buckets:
                acc = jnp.where(bkt == b, tab_ref[b, head0 + h], acc)
            bias_ref[h] = acc

    ki = lax.broadcasted_iota(I32, (bb, 2 * bb), 1)
    kvalid = (ki >= bb) | (n > 0)
    for h in range(hpg):
        sl = slice(h * hd, (h + 1) * hd)
        q = _rms(q_ref[:, sl].astype(F32), qn_ref[...]) * scale
        k = jnp.concatenate([kp_ref[:, sl], kc_ref[:, sl]], axis=0).astype(F32)
        k = _rms(k, kn_ref[...])
        v = jnp.concatenate([vp_ref[:, sl], vc_ref[:, sl]], axis=0)
        s = lax.dot_general(q.astype(BF16), k.astype(BF16), (((1,), (1,)), ((), ())),
                            preferred_element_type=F32)
        s = jnp.where(kvalid, s + bias_ref[h], NEG)
        m = jnp.max(s, axis=-1, keepdims=True)
        e = jnp.exp(s - m)
        l = jnp.sum(e, axis=-1, keepdims=True)
        o = jnp.dot((e / l).astype(BF16), v, preferred_element_type=F32)
        o_ref[:, sl] = o.astype(o_ref.dtype)
        lse_ref[:, sl] = jnp.broadcast_to(m + jnp.log(l), (bb, hd))


def band_attention(proj_a, rel_bias, qn3, kn3, l, g, cfg):
    S, npa = proj_a.shape
    win, dil = cfg.a_groups[g]
    steps = win // dil
    bb, hpg, hd = cfg.band_block, cfg.heads_per_group, cfg.hd_a
    gw = hpg * hd
    ng = len(cfg.a_groups)
    ls = S // dil
    assert S % dil == 0 and ls % bb == 0 and steps <= bb
    nblk = ls // bb
    nb = npa // gw
    x = proj_a.reshape(ls, dil * npa)

    qi = np.arange(bb)[:, None]
    ki = np.arange(2 * bb)[None, :]
    rel = qi + bb - ki
    inside = (rel >= 0) & (rel <= steps)
    bkt = np.where(inside, _t5_bucket_np(np.maximum(rel, 0) * dil, cfg.n_buckets, cfg.max_dist), -1)
    buckets = tuple(int(b) for b in np.unique(bkt[inside]))

    def qmap(p, n):
        return (n, p * nb + g)

    def kmap_c(p, n):
        return (n, p * nb + ng + g)

    def kmap_p(p, n):
        return (jnp.maximum(n - 1, 0), p * nb + ng + g)

    def vmap_c(p, n):
        return (n, p * nb + 2 * ng + g)

    def vmap_p(p, n):
        return (jnp.maximum(n - 1, 0), p * nb + 2 * ng + g)

    blk = (bb, gw)
    o, lse = pl.pallas_call(
        functools.partial(_band_kernel, hpg=hpg, hd=hd, head0=g * hpg, buckets=buckets,
                          scale=float(hd) ** -0.5),
        grid=(dil, nblk),
        in_specs=[pl.BlockSpec(memory_space=pltpu.SMEM),
                  pl.BlockSpec((bb, 2 * bb), lambda p, n: (0, 0)),
                  pl.BlockSpec(blk, qmap), pl.BlockSpec(blk, kmap_p), pl.BlockSpec(blk, kmap_c),
                  pl.BlockSpec(blk, vmap_p), pl.BlockSpec(blk, vmap_c),
                  pl.BlockSpec((None, 1, hd), lambda p, n: (l, 0, 0)),
                  pl.BlockSpec((None, 1, hd), lambda p, n: (l, 0, 0))],
        out_specs=[pl.BlockSpec(blk, lambda p, n: (n, p)), pl.BlockSpec(blk, lambda p, n: (n, p))],
        out_shape=[jax.ShapeDtypeStruct((ls, dil * gw), BF16),
                   jax.ShapeDtypeStruct((ls, dil * gw), F32)],
        scratch_shapes=[pltpu.VMEM((hpg, bb, 2 * bb), F32)],
        compiler_params=_cparams(2, 32),
        name=f"band_attn_g{g}",
    )(rel_bias, jnp.asarray(bkt, I32), x, x, x, x, x, qn3, kn3)
    return o.reshape(S, gw), lse.reshape(S, gw)


def _alpha_kernel(*refs, ng, gw):
    o_refs, l_refs, y_ref = refs[:ng], refs[ng:2 * ng], refs[2 * ng]
    ls = [r[...] for r in l_refs]
    m = functools.reduce(jnp.maximum, ls)
    es = [jnp.exp(v - m) for v in ls]
    den = functools.reduce(lambda a, b: a + b, es)
    for g in range(ng):
        y_ref[:, g * gw:(g + 1) * gw] = (es[g] / den * o_refs[g][...].astype(F32)).astype(y_ref.dtype)


def alpha_merge(outs, lses, tm=512):
    ng = len(outs)
    S, gw = outs[0].shape
    tm = min(tm, S)
    spec = pl.BlockSpec((tm, gw), lambda i: (i, 0))
    return pl.pallas_call(
        functools.partial(_alpha_kernel, ng=ng, gw=gw),
        grid=(S // tm,),
        in_specs=[spec] * (2 * ng),
        out_specs=pl.BlockSpec((tm, ng * gw), lambda i: (i, 0)),
        out_shape=jax.ShapeDtypeStruct((S, ng * gw), BF16),
        compiler_params=_cparams(1, 32),
        name="alpha_merge",
    )(*outs, *lses)


def _cross_kernel(q_ref, k_ref, v_ref, qn_ref, kn_ref, o_ref, *, scale):
    q = _rms(q_ref[...].astype(F32), qn_ref[...]) * scale
    k = _rms(k_ref[...].astype(F32), kn_ref[...])
    s = lax.dot_general(q.astype(BF16), k.astype(BF16), (((1,), (1,)), ((), ())),
                        preferred_element_type=F32)
    m = jnp.max(s, axis=-1, keepdims=True)
    e = jnp.exp(s - m)
    pr = e / jnp.sum(e, axis=-1, keepdims=True)
    o_ref[...] = jnp.dot(pr.astype(BF16), v_ref[...], preferred_element_type=F32).astype(o_ref.dtype)


def cross_attention(proj_t, q_col0, kv, qn3, kn3, l, cfg, tm=1024):
    S = proj_t.shape[0]
    mlen = kv.shape[0]
    hm, hd = cfg.h_m, cfg.hd_m
    tm = min(tm, S)
    assert q_col0 % hd == 0
    qoff = q_col0 // hd
    return pl.pallas_call(
        functools.partial(_cross_kernel, scale=float(hd) ** -0.5),
        grid=(S // tm, hm),
        in_specs=[pl.BlockSpec((tm, hd), lambda i, h: (i, qoff + h)),
                  pl.BlockSpec((mlen, hd), lambda i, h: (0, h)),
                  pl.BlockSpec((mlen, hd), lambda i, h: (0, hm + h)),
                  pl.BlockSpec((None, 1, hd), lambda i, h: (l, 0, 0)),
                  pl.BlockSpec((None, 1, hd), lambda i, h: (l, 0, 0))],
        out_specs=pl.BlockSpec((tm, hd), lambda i, h: (i, h)),
        out_shape=jax.ShapeDtypeStruct((S, hm * hd), BF16),
        compiler_params=_cparams(2, 32),
        name="cross_attn",
    )(proj_t, kv, kv, qn3, kn3)


def _shift_rows(x, tail, s):
    rolled = pltpu.roll(x, s, axis=0)
    row = lax.broadcasted_iota(I32, (SUBLANES, x.shape[1]), 0)
    head = jnp.where(row >= s, rolled[:SUBLANES], pltpu.roll(tail, s, axis=0))
    return jnp.concatenate([head, rolled[SUBLANES:]], axis=0)


def _mlstm_kernel(q_ref, k_ref, v_ref, g_ref, ob_ref, cw_ref, cb_ref, gb_ref, hn_ref,
                  y_ref, cn_ref, ms_ref, tail_ref, *, nh, hd, dp, conv_w):
    c = pl.program_id(0)
    L = q_ref.shape[0]
    wb = nh * hd

    @pl.when(c == 0)
    def _():
        cn_ref[...] = jnp.zeros_like(cn_ref)
        ms_ref[...] = jnp.zeros_like(ms_ref)
        tail_ref[...] = jnp.zeros_like(tail_ref)

    def conv_silu(x_ref, col0):
        x = x_ref[...].astype(F32)
        tail = tail_ref[:, col0:col0 + wb]
        y = cb_ref[:, col0:col0 + wb] + cw_ref[conv_w - 1:conv_w, col0:col0 + wb] * x
        for s in range(1, conv_w):
            y = y + cw_ref[conv_w - 1 - s:conv_w - s, col0:col0 + wb] * _shift_rows(x, tail, s)
        tail_ref[:, col0:col0 + wb] = x[L - SUBLANES:]
        return y * _sigmoid(y)

    qs = conv_silu(q_ref, 0)
    ks = conv_silu(k_ref, wb) * (float(hd) ** -0.5)

    G = g_ref[...] + gb_ref[...]
    lf = jnp.minimum(G, 0.0) - jnp.log(1.0 + jnp.exp(-jnp.abs(G)))
    row = lax.broadcasted_iota(I32, (L, LANES), 0)
    F = lf
    sh = 1
    while sh < L:
        F = F + jnp.where(row >= sh, pltpu.roll(F, sh, axis=0), 0.0)
        sh *= 2
    GT = G.T
    FT = F.T
    ti = lax.broadcasted_iota(I32, (L, L), 0)
    si = lax.broadcasted_iota(I32, (L, L), 1)
    causal = ti >= si
    ones_col = (lax.broadcasted_iota(I32, (L, dp - hd), 1) == 0).astype(BF16)
    ms = ms_ref[...]

    for h in range(nh):
        sl = slice(h * hd, (h + 1) * hd)
        li_c, F_c = G[:, h:h + 1], F[:, nh + h:nh + h + 1]
        li_r, F_r = GT[h:h + 1, :], FT[nh + h:nh + h + 1, :]
        F_last = F_c[L - 1:L, :]
        m_prev = ms[:, h:h + 1]
        a_r = F_last - F_r + li_r
        b = jnp.max(a_r, axis=-1, keepdims=True)
        ea_c = jnp.exp(F_last - F_c + li_c - b)
        logw = jnp.where(causal, F_c - F_r + li_r, NEG)
        m_intra = jnp.max(logw, axis=-1, keepdims=True)
        m_inter = F_c + m_prev
        m_t = jnp.maximum(m_inter, m_intra)
        q = qs[:, sl].astype(BF16)
        k = ks[:, sl]
        v_aug = jnp.concatenate([v_ref[:, sl], ones_col], axis=1)
        s = lax.dot_general(q, k.astype(BF16), (((1,), (1,)), ((), ())),
                            preferred_element_type=F32) * jnp.exp(logw - m_t)
        inter = jnp.exp(m_inter - m_t)
        cn = cn_ref[h]
        num = (jnp.dot(s.astype(BF16), v_aug, preferred_element_type=F32)
               + inter * jnp.dot(q, cn.astype(BF16), preferred_element_type=F32))
        den = num[:, hd:hd + 1]
        hv = num[:, :hd] / jnp.maximum(jnp.abs(den), jnp.exp(-m_t))
        m_new = jnp.maximum(F_last + m_prev, b)
        decay = jnp.exp(F_last + m_prev - m_new)
        inj = jnp.exp(b - m_new)
        kv = jnp.dot((ea_c * k).T.astype(BF16), v_aug, preferred_element_type=F32)
        cn_ref[h] = decay * cn + inj * kv
        ms_ref[:, h:h + 1] = m_new
        hb = _rms(hv, hn_ref[:, sl])
        y_ref[:, sl] = (_sigmoid(ob_ref[:, sl].astype(F32)) * hb).astype(y_ref.dtype)


def mlstm(proj_b, gates, proj_t, conv_w, conv_b3, gbias3, hnorm3, l, cfg):
    S = proj_b.shape[0]
    L, nh, hd = cfg.chunk, cfg.h_b, cfg.hd_b
    wb = nh * hd
    dp = _round_up(hd + 1, LANES)
    assert L == LANES and S % L == 0 and 2 * nh <= LANES and cfg.conv_w <= SUBLANES
    cw = conv_w.shape[1]
    return pl.pallas_call(
        functools.partial(_mlstm_kernel, nh=nh, hd=hd, dp=dp, conv_w=cw),
        grid=(S // L,),
        in_specs=[pl.BlockSpec((L, wb), lambda c: (c, 0)),
                  pl.BlockSpec((L, wb), lambda c: (c, 1)),
                  pl.BlockSpec((L, wb), lambda c: (c, 2)),
                  pl.BlockSpec((L, LANES), lambda c: (c, 0)),
                  pl.BlockSpec((L, wb), lambda c: (c, 0)),
                  pl.BlockSpec((None, cw, 2 * wb), lambda c: (l, 0, 0)),
                  pl.BlockSpec((None, 1, 2 * wb), lambda c: (l, 0, 0)),
                  pl.BlockSpec((None, 1, LANES), lambda c: (l, 0, 0)),
                  pl.BlockSpec((None, 1, wb), lambda c: (l, 0, 0))],
        out_specs=pl.BlockSpec((L, wb), lambda c: (c, 0)),
        out_shape=jax.ShapeDtypeStruct((S, wb), BF16),
        scratch_shapes=[pltpu.VMEM((nh, hd, dp), F32), pltpu.VMEM((1, LANES), F32),
                        pltpu.VMEM((SUBLANES, 2 * wb), F32)],
        compiler_params=_cparams(1, 40),
        name="mlstm",
    )(proj_b, proj_b, proj_b, gates, proj_t, conv_w, conv_b3, gbias3, hnorm3)


def _merge_kernel(ya_ref, yb_ref, ym_ref, wa_ref, wb_ref, wm_ref, ga_ref, gb_ref, gm_ref,
                  o_ref, was, wbs, wms):
    @pl.when(pl.program_id(1) == 0)
    def _():
        was[...] = wa_ref[...].astype(BF16)
        wbs[...] = wb_ref[...].astype(BF16)
        wms[...] = wm_ref[...].astype(BF16)

    def branch(y_ref, w_ref, g_ref):
        return _sigmoid(g_ref[...].astype(F32)) * jnp.dot(y_ref[...], w_ref[...],
                                                          preferred_element_type=F32)

    o_ref[...] = (branch(ya_ref, was, ga_ref) + branch(yb_ref, wbs, gb_ref)
                  + branch(ym_ref, wms, gm_ref)).astype(o_ref.dtype)


def gated_merge(ya, yb, ym, w_a, w_b, w_m, proj_t, gate_col0, l, d, tm=1024, tn=512):
    S = ya.shape[0]
    tm = min(tm, S)
    while gate_col0 % tn or d % tn:
        tn //= 2
    goff = gate_col0 // tn
    nd = d // tn

    def lhs_spec(y):
        return pl.BlockSpec((tm, y.shape[1]), lambda n, m: (m, 0))

    def w_spec(w):
        return pl.BlockSpec((None, w.shape[1], tn), lambda n, m: (l, 0, n))

    def g_spec(j):
        return pl.BlockSpec((tm, tn), lambda n, m: (m, goff + j * nd + n))

    ksum = ya.shape[1] + yb.shape[1] + ym.shape[1]
    vmem = 2 * tm * ksum * 2 + 2 * ksum * tn * 4 + ksum * tn * 2 + 8 * tm * tn * 2 + 4 * tm * tn * 4
    return pl.pallas_call(
        _merge_kernel,
        grid=(nd, S // tm),
        in_specs=[lhs_spec(ya), lhs_spec(yb), lhs_spec(ym), w_spec(w_a), w_spec(w_b), w_spec(w_m),
                  g_spec(0), g_spec(1), g_spec(2)],
        out_specs=pl.BlockSpec((tm, tn), lambda n, m: (m, n)),
        out_shape=jax.ShapeDtypeStruct((S, d), BF16),
        scratch_shapes=[pltpu.VMEM((w.shape[1], tn), BF16) for w in (w_a, w_b, w_m)],
        compiler_params=_cparams(2, vmem // MIB + 6),
        name="gated_merge",
    )(ya, yb, ym, w_a, w_b, w_m, proj_t, proj_t, proj_t)


def _router_kernel(x_ref, g_ref, wr_ref, br_ref, h_ref, meta_ref, gate_ref, cnt_ref, run_ref, *, n_exp):
    i = pl.program_id(0)
    tm = x_ref.shape[0]

    @pl.when(i == 0)
    def _():
        run_ref[...] = jnp.zeros_like(run_ref)

    y = _rms(x_ref[...], g_ref[...])
    h_ref[...] = y
    logits = jnp.dot(y, wr_ref[...], preferred_element_type=F32,
                     precision=lax.Precision.HIGHEST) + br_ref[...]
    lane = lax.broadcasted_iota(I32, (tm, LANES), 1)
    lanef = lane.astype(F32)
    logits = jnp.where(lane < n_exp, logits, -jnp.inf)
    v1 = jnp.max(logits, axis=-1, keepdims=True)
    i1 = jnp.min(jnp.where(logits == v1, lanef, float(LANES)), axis=-1, keepdims=True).astype(I32)
    rest = jnp.where(lane == i1, -jnp.inf, logits)
    v2 = jnp.max(rest, axis=-1, keepdims=True)
    i2 = jnp.min(jnp.where(rest == v2, lanef, float(LANES)), axis=-1, keepdims=True).astype(I32)
    e = jnp.exp(v2 - v1)
    g1 = 1.0 / (1.0 + e)
    g2 = e / (1.0 + e)
    oh1 = lane == i1
    oh2 = lane == i2
    oh = jnp.where(oh1 | oh2, 1.0, 0.0)
    r = lax.broadcasted_iota(I32, (tm, tm), 0)
    cidx = lax.broadcasted_iota(I32, (tm, tm), 1)
    tri = jnp.where(cidx < r, 1.0, 0.0).astype(BF16)
    cum = jnp.dot(tri, oh.astype(BF16), preferred_element_type=F32) + run_ref[...]
    r1 = jnp.sum(jnp.where(oh1, cum, 0.0), axis=-1, keepdims=True).astype(I32)
    r2 = jnp.sum(jnp.where(oh2, cum, 0.0), axis=-1, keepdims=True).astype(I32)
    run_ref[...] += jnp.sum(oh, axis=0, keepdims=True)
    meta_ref[...] = jnp.where(lane == 0, i1, jnp.where(lane == 1, i2,
                              jnp.where(lane == 2, r1, jnp.where(lane == 3, r2, 0))))
    gate_ref[...] = jnp.where(lane == 0, g1, jnp.where(lane == 1, g2, 0.0))
    cnt_ref[...] = run_ref[...]


def norm_router(x, g3, w_router_p, b_router_p, l, lr, n_exp, tm=256):
    S, D = x.shape
    tm = min(tm, S)
    return pl.pallas_call(
        functools.partial(_router_kernel, n_exp=n_exp),
        grid=(S // tm,),
        in_specs=[pl.BlockSpec((tm, D), lambda i: (i, 0)),
                  pl.BlockSpec((None, 1, D), lambda i: (l, 0, 0)),
                  pl.BlockSpec((None, D, LANES), lambda i: (lr, 0, 0)),
                  pl.BlockSpec((None, 1, LANES), lambda i: (lr, 0, 0))],
        out_specs=[pl.BlockSpec((tm, D), lambda i: (i, 0)),
                   pl.BlockSpec((tm, LANES), lambda i: (i, 0)),
                   pl.BlockSpec((tm, LANES), lambda i: (i, 0)),
                   pl.BlockSpec((1, LANES), lambda i: (0, 0))],
        out_shape=[jax.ShapeDtypeStruct((S, D), F32), jax.ShapeDtypeStruct((S, LANES), I32),
                   jax.ShapeDtypeStruct((S, LANES), F32), jax.ShapeDtypeStruct((1, LANES), F32)],
        scratch_shapes=[pltpu.VMEM((1, LANES), F32)],
        compiler_params=_cparams(1, 40),
        name="norm_router",
    )(x, g3, w_router_p, b_router_p)


def _row_copy(src_hbm, dst, src_row, dst_row, sem):
    return pltpu.make_async_copy(src_hbm.at[pl.ds(src_row, 1)], dst.at[pl.ds(dst_row, 1)], sem)


def _gather_kernel(order_ref, nv_ref, h_hbm, xs_ref, buf, sem, *, tg):
    i = pl.program_id(0)

    @pl.when(i < nv_ref[0])
    def _():
        def start(r, carry):
            _row_copy(h_hbm, buf, order_ref[i * tg + r], r, sem).start()
            return carry

        def wait(r, carry):
            _row_copy(h_hbm, buf, 0, r, sem).wait()
            return carry

        lax.fori_loop(0, tg, start, 0)
        lax.fori_loop(0, tg, wait, 0)
        xs_ref[...] = buf[...].astype(BF16)

    @pl.when(i >= nv_ref[0])
    def _():
        xs_ref[...] = jnp.zeros_like(xs_ref)


def gather_rows(h, order, nvalid, n_tiles, tg):
    D = h.shape[1]
    grid_spec = pltpu.PrefetchScalarGridSpec(
        num_scalar_prefetch=2,
        grid=(n_tiles,),
        in_specs=[pl.BlockSpec(memory_space=pl.ANY)],
        out_specs=pl.BlockSpec((tg, D), lambda i, order, nv: (i, 0)),
        scratch_shapes=[pltpu.VMEM((tg, D), F32), pltpu.SemaphoreType.DMA(())],
    )
    return pl.pallas_call(
        functools.partial(_gather_kernel, tg=tg),
        grid_spec=grid_spec,
        out_shape=jax.ShapeDtypeStruct((n_tiles * tg, D), BF16),
        compiler_params=_cparams(1, 40),
        name="moe_gather",
    )(order, nvalid, h)


def _gmm_kernel(te_ref, src_ref, first_ref, nv_ref, *refs, nw, epi):
    x_ref = refs[0]
    w_refs = refs[1:1 + nw]
    out_ref = refs[1 + nw]
    ws_refs = refs[2 + nw:2 + 2 * nw]
    i = pl.program_id(1)

    @pl.when(first_ref[i] == 1)
    def _():
        for w_ref, ws_ref in zip(w_refs, ws_refs):
            ws_ref[...] = w_ref[...].astype(BF16)

    @pl.when(i < nv_ref[0])
    def _():
        x = x_ref[...]
        accs = [jnp.dot(x, ws_ref[...], preferred_element_type=F32) for ws_ref in ws_refs]
        if epi == "swiglu":
            g, u = accs
            out = g * _sigmoid(g) * u
        else:
            out = accs[0]
        out_ref[...] = out.astype(out_ref.dtype)

    @pl.when(i >= nv_ref[0])
    def _():
        out_ref[...] = jnp.zeros_like(out_ref)


def grouped_matmul(xs, ws, lr, tile_meta, *, epi, out_dtype, tg, tn, name):
    te, src, first, nvalid = tile_meta
    P, K = xs.shape
    N = ws[0].shape[-1]
    nw = len(ws)
    tn = min(tn, N)
    assert N % tn == 0 and P % tg == 0
    w_spec = pl.BlockSpec((None, None, K, tn), lambda n, i, te, src, first, nv: (lr, te[i], 0, n))
    grid_spec = pltpu.PrefetchScalarGridSpec(
        num_scalar_prefetch=4,
        grid=(N // tn, P // tg),
        in_specs=[pl.BlockSpec((tg, K), lambda n, i, te, src, first, nv: (src[i], 0))] + [w_spec] * nw,
        out_specs=pl.BlockSpec((tg, tn), lambda n, i, te, src, first, nv: (i, n)),
        scratch_shapes=[pltpu.VMEM((K, tn), BF16)] * nw,
    )
    obytes = jnp.dtype(out_dtype).itemsize
    vmem = 2 * tg * K * 2 + nw * (2 * K * tn * 4 + K * tn * 2) + 2 * tg * tn * obytes + (nw + 1) * tg * tn * 4
    return pl.pallas_call(
        functools.partial(_gmm_kernel, nw=nw, epi=epi),
        grid_spec=grid_spec,
        out_shape=jax.ShapeDtypeStruct((P, N), out_dtype),
        compiler_params=_cparams(2, vmem // MIB + 6),
        name=name,
    )(te, src, first, nvalid, xs, *ws)


def _combine_kernel(dest_ref, x_ref, gate_ref, ys_hbm, out_ref, buf, sem, *, tc, top_k):
    i = pl.program_id(0)

    def start(r, carry):
        for k in range(top_k):
            _row_copy(ys_hbm, buf.at[k], dest_ref[(i * tc + r) * top_k + k], r, sem).start()
        return carry

    def wait(r, carry):
        for k in range(top_k):
            _row_copy(ys_hbm, buf.at[k], 0, r, sem).wait()
        return carry

    lax.fori_loop(0, tc, start, 0)
    lax.fori_loop(0, tc, wait, 0)
    g = gate_ref[...]
    out = x_ref[...]
    for k in range(top_k):
        out = out + g[:, k:k + 1] * buf[k]
    out_ref[...] = out


def moe_combine(x, gates, ys, dest_flat, top_k, tc=256):
    S, D = x.shape
    tc = min(tc, S)
    grid_spec = pltpu.PrefetchScalarGridSpec(
        num_scalar_prefetch=1,
        grid=(S // tc,),
        in_specs=[pl.BlockSpec((tc, D), lambda i, d: (i, 0)),
                  pl.BlockSpec((tc, LANES), lambda i, d: (i, 0)),
                  pl.BlockSpec(memory_space=pl.ANY)],
        out_specs=pl.BlockSpec((tc, D), lambda i, d: (i, 0)),
        scratch_shapes=[pltpu.VMEM((top_k, tc, D), F32), pltpu.SemaphoreType.DMA(())],
    )
    return pl.pallas_call(
        functools.partial(_combine_kernel, tc=tc, top_k=top_k),
        grid_spec=grid_spec,
        out_shape=jax.ShapeDtypeStruct((S, D), F32),
        compiler_params=_cparams(1, 48),
        name="moe_combine",
    )(dest_flat, x, gates, ys)


def moe_layer(x, norm_ffn3, l, w_router, b_router, w_e_gate, w_e_up, w_e_down, lr, cfg):
    S, D = x.shape
    E, top_k, tg = cfg.n_experts, 2, cfg.tg
    tg = min(tg, S)
    wr = jnp.pad(w_router, ((0, 0), (0, 0), (0, LANES - E)))
    br = jnp.pad(b_router, ((0, 0), (0, LANES - E)))[:, None, :]
    h, meta, gates, cnt = norm_router(x, norm_ffn3, wr, br, l, lr, E)

    eid, rank = meta[:, 0:top_k], meta[:, top_k:2 * top_k]
    counts = cnt[0, :E].astype(I32)
    padded = (counts + tg - 1) // tg * tg
    ends = jnp.cumsum(padded)
    dest = (ends - padded)[eid] + rank
    n_tiles = (S * top_k) // tg + E
    order = jnp.zeros((n_tiles * tg,), I32).at[dest.reshape(-1)].set(
        jnp.repeat(jnp.arange(S, dtype=I32), top_k))
    nvalid = (ends[-1] // tg).astype(I32)
    tile = jnp.arange(n_tiles, dtype=I32)
    src = jnp.minimum(tile, nvalid - 1)
    te = jnp.sum((src * tg)[:, None] >= ends[None, :], axis=1).astype(I32)
    first = ((tile == 0) | (te != jnp.roll(te, 1))).astype(I32)
    nv = nvalid.reshape(1)
    tile_meta = (te, src, first, nv)

    xs = gather_rows(h, order, nv, n_tiles, tg)
    a = grouped_matmul(xs, [w_e_gate, w_e_up], lr, tile_meta, epi="swiglu", out_dtype=BF16,
                       tg=tg, tn=cfg.tn2, name="moe_gate_up")
    ys = grouped_matmul(a, [w_e_down], lr, tile_meta, epi="plain", out_dtype=F32,
                        tg=tg, tn=cfg.tn, name="moe_down")
    return moe_combine(x, gates, ys, dest.reshape(-1), top_k)


def _forward(x, mem, rel_bias, norm_mix, norm_ffn, norm_mem, w_in, qn_a, kn_a, conv_w, conv_b,
             gate_bias_b, hnorm_b, w_mem_kv, qn_m, kn_m, w_br_a, w_br_b, w_br_m, w_out,
             w_ff_gate, w_ff_up, w_ff_down, w_router, b_router, w_e_gate, w_e_up, w_e_down, cfg):
    B, S, D = x.shape
    assert B == 1 and mem.shape[0] == 1
    depth = norm_mix.shape[0]
    ng = len(cfg.a_groups)
    w_a = ng * cfg.heads_per_group * cfg.hd_a
    w_b = cfg.h_b * cfg.hd_b
    w_m = cfg.h_m * cfg.hd_m
    if_col0 = 3 * w_a + 3 * w_b
    tail_col0 = if_col0 + 2 * cfg.h_b
    n_tail = w_b + w_m + 3 * D
    d_ff = w_ff_gate.shape[-1]
    d_ff_p = _round_up(d_ff, cfg.tk_down) if d_ff > cfg.tk_down else _round_up(d_ff, LANES)

    def row3(p):
        return p[:, None, :]

    x = x.reshape(S, D)
    mem2 = mem.reshape(mem.shape[1], D)
    norm_mix3, norm_ffn3, norm_mem3 = row3(norm_mix), row3(norm_ffn), row3(norm_mem)
    qn_a3, kn_a3, qn_m3, kn_m3 = row3(qn_a), row3(kn_a), row3(qn_m), row3(kn_m)
    conv_b3, hnorm3 = row3(conv_b), row3(hnorm_b)
    gbias3 = row3(jnp.pad(gate_bias_b, ((0, 0), (0, LANES - 2 * cfg.h_b))))
    mm = functools.partial(matmul, tm=cfg.tm)

    for l in range(depth):
        h = rmsnorm(x, norm_mix3, l)
        proj_a = mm(h, [w_in], lead=l, col0=0, n_out=3 * w_a, tn=cfg.tn, name="proj_a")
        proj_b = mm(h, [w_in], lead=l, col0=3 * w_a, n_out=3 * w_b, tn=cfg.tn, name="proj_b")
        gates_b = mm(h, [w_in], lead=l, col0=if_col0, n_out=LANES, out_dtype=F32, tn=LANES,
                     name="proj_if")
        w_tail = w_in[l, :, tail_col0:].astype(BF16)
        proj_t = mm(h, [w_tail], n_out=n_tail, tn=cfg.tn, name="proj_tail")

        outs, lses = zip(*[band_attention(proj_a, rel_bias, qn_a3, kn_a3, l, g, cfg) for g in range(ng)])
        y_a = alpha_merge(outs, lses)
        y_b = mlstm(proj_b, gates_b, proj_t, conv_w, conv_b3, gbias3, hnorm3, l, cfg)
        hm = rmsnorm(mem2, norm_mem3, l)
        kv = mm(hm, [w_mem_kv], lead=l, n_out=2 * w_m, tn=cfg.tn, name="mem_kv")
        y_m = cross_attention(proj_t, w_b, kv, qn_m3, kn_m3, l, cfg)
        y = gated_merge(y_a, y_b, y_m, w_br_a, w_br_b, w_br_m, proj_t, w_b + w_m, l, D,
                        tm=cfg.tm // 2, tn=cfg.tn)
        x = mm(y, [w_out], lead=l, n_out=D, epi="residual", res=x, out_dtype=F32, tn=cfg.tn,
               name="out_proj")

        if l % 2 == 0:
            ld = l // 2
            h2 = rmsnorm(x, norm_ffn3, l)
            a = mm(h2, [w_ff_gate, w_ff_up], lead=ld, n_out=d_ff_p, epi="swiglu", n_valid=d_ff,
                   tn=cfg.tn2, name="ffn_gate_up")
            wd = jnp.pad(w_ff_down[ld], ((0, d_ff_p - d_ff), (0, 0))).astype(BF16)
            x = matmul_ktiled_residual(a, wd, x, tm=cfg.tm, tn=1024, tk=cfg.tk_down, name="ffn_down")
        else:
            x = moe_layer(x, norm_ffn3, l, w_router, b_router, w_e_gate, w_e_up, w_e_down, l // 2, cfg)
    return x.reshape(B, S, D)


def kernel(x, mem, rel_bias, norm_mix, norm_ffn, norm_mem, w_in, qn_a, kn_a, conv_w, conv_b, gate_bias_b, hnorm_b, w_mem_kv, qn_m, kn_m, w_br_a, w_br_b, w_br_m, w_out, w_ff_gate, w_ff_up, w_ff_down, w_router, b_router, w_e_gate, w_e_up, w_e_down):
    return _forward(x, mem, rel_bias, norm_mix, norm_ffn, norm_mem, w_in, qn_a, kn_a, conv_w, conv_b,
                    gate_bias_b, hnorm_b, w_mem_kv, qn_m, kn_m, w_br_a, w_br_b, w_br_m, w_out,
                    w_ff_gate, w_ff_up, w_ff_down, w_router, b_router, w_e_gate, w_e_up, w_e_down,
                    Cfg())
```

```python
import functools
from typing import NamedTuple

import numpy as np
import jax
import jax.numpy as jnp
from jax import lax
from jax.experimental import pallas as pl
from jax.experimental.pallas import tpu as pltpu

F32 = jnp.float32
BF16 = jnp.bfloat16
I32 = jnp.int32
EPS = 1e-6
NEG = -1e30
MIB = 1 << 20
LANES = 128
SUBLANES = 8
VMEM_CAP_MIB = 60


class Cfg(NamedTuple):
    a_groups: tuple = ((128, 1), (512, 4), (2048, 16))
    heads_per_group: int = 4
    hd_a: int = 128
    band_block: int = 128
    h_b: int = 4
    hd_b: int = 384
    chunk: int = 128
    conv_w: int = 4
    h_m: int = 4
    hd_m: int = 256
    n_buckets: int = 32
    max_dist: int = 2048
    n_experts: int = 8
    tm: int = 1024
    tn: int = 512
    tn2: int = 256
    tg: int = 512
    tk_down: int = 2816


def _cparams(n_axes, vmem_mib):
    return pltpu.CompilerParams(dimension_semantics=("arbitrary",) * n_axes,
                                vmem_limit_bytes=int(min(vmem_mib, VMEM_CAP_MIB)) * MIB)


def _round_up(a, b):
    return -(-a // b) * b


def _sigmoid(x):
    return 1.0 / (1.0 + jnp.exp(-x))


def _rms(x, g):
    return x * lax.rsqrt(jnp.mean(x * x, axis=-1, keepdims=True) + EPS) * g


def _rmsnorm_kernel(x_ref, g_ref, o_ref):
    o_ref[...] = _rms(x_ref[...].astype(F32), g_ref[...]).astype(o_ref.dtype)


def rmsnorm(x, g3, l, out_dtype=BF16, tm=256):
    M, D = x.shape
    tm = min(tm, M)
    return pl.pallas_call(
        _rmsnorm_kernel,
        grid=(M // tm,),
        in_specs=[pl.BlockSpec((tm, D), lambda i: (i, 0)),
                  pl.BlockSpec((None, 1, D), lambda i: (l, 0, 0))],
        out_specs=pl.BlockSpec((tm, D), lambda i: (i, 0)),
        out_shape=jax.ShapeDtypeStruct((M, D), out_dtype),
        compiler_params=_cparams(1, 32),
        name="rmsnorm",
    )(x, g3)


def _mm_kernel(*refs, nw, cast, trans, epi, n_valid, tn):
    lhs_ref = refs[0]
    w_refs = refs[1:1 + nw]
    pos = 1 + nw
    res_ref = None
    if epi == "residual":
        res_ref = refs[pos]
        pos += 1
    out_ref = refs[pos]
    pos += 1
    ws_refs = refs[pos:pos + nw] if cast else w_refs

    if cast:
        @pl.when(pl.program_id(1) == 0)
        def _():
            for w_ref, ws_ref in zip(w_refs, ws_refs):
                ws_ref[...] = w_ref[...].astype(BF16)

    lhs = lhs_ref[...]
    dims = (((1,), (1,)), ((), ())) if trans else (((1,), (0,)), ((), ()))
    accs = [lax.dot_general(lhs, ws_ref[...], dims, preferred_element_type=F32) for ws_ref in ws_refs]
    if epi == "plain":
        out = accs[0]
    elif epi == "residual":
        out = res_ref[...] + accs[0]
    else:
        g, u = accs
        out = g * _sigmoid(g) * u
        if n_valid is not None:
            col = pl.program_id(0) * tn + lax.broadcasted_iota(I32, out.shape, 1)
            out = jnp.where(col < n_valid, out, 0.0)
    out_ref[...] = out.astype(out_ref.dtype)


def matmul(lhs, ws, *, lead=None, col0=0, n_out, epi="plain", res=None, n_valid=None,
           trans=False, blk_of=None, out_dtype=BF16, tm=1024, tn=512, name="matmul"):
    M, K = lhs.shape
    nw = len(ws)
    cast = ws[0].dtype != BF16
    tm = min(tm, M)
    tn = min(tn, n_out)
    if not trans:
        while col0 % tn:
            tn //= 2
    assert tn >= LANES and M % tm == 0
    gn = pl.cdiv(n_out, tn)
    if trans and col0 % tn == 0:
        assert lead is not None and n_out % tn == 0
        if blk_of is None:
            def blk_of(n):
                return col0 // tn + n
        w_spec = pl.BlockSpec((None, tn, K), lambda n, m: (lead, blk_of(n), 0))
        w_block = (tn, K)
    elif trans:
        n_rows = ws[0].shape[1]
        assert lead is not None and n_out % tn == 0 and blk_of is None
        assert n_rows % SUBLANES == 0 and col0 % SUBLANES == 0 and tn % SUBLANES == 0
        row8 = (lead * n_rows + col0) // SUBLANES
        ws = [w.reshape(-1, K) for w in ws]
        w_spec = pl.BlockSpec((pl.Element(tn), pl.Element(K)),
                              lambda n, m: ((row8 + n * (tn // SUBLANES)) * SUBLANES, 0))
        w_block = (tn, K)
    else:
        off = col0 // tn
        wlast = pl.cdiv(ws[0].shape[-1], tn) - 1
        if lead is None:
            w_spec = pl.BlockSpec((K, tn), lambda n, m: (0, jnp.minimum(n + off, wlast)))
        else:
            w_spec = pl.BlockSpec((None, K, tn), lambda n, m: (lead, 0, jnp.minimum(n + off, wlast)))
        w_block = (K, tn)
    in_specs = [pl.BlockSpec((tm, K), lambda n, m: (m, 0))] + [w_spec] * nw
    args = [lhs] + list(ws)
    if epi == "residual":
        in_specs.append(pl.BlockSpec((tm, tn), lambda n, m: (m, n)))
        args.append(res)
    wbytes = ws[0].dtype.itemsize
    obytes = jnp.dtype(out_dtype).itemsize
    vmem = (2 * tm * K * 2 + nw * 2 * K * tn * wbytes + (nw * K * tn * 2 if cast else 0)
            + 2 * tm * tn * obytes + (2 * tm * tn * 4 if epi == "residual" else 0)
            + (nw + 1) * tm * tn * 4)
    return pl.pallas_call(
        functools.partial(_mm_kernel, nw=nw, cast=cast, trans=trans, epi=epi, n_valid=n_valid, tn=tn),
        grid=(gn, M // tm),
        in_specs=in_specs,
        out_specs=pl.BlockSpec((tm, tn), lambda n, m: (m, n)),
        out_shape=jax.ShapeDtypeStruct((M, n_out), out_dtype),
        scratch_shapes=[pltpu.VMEM(w_block, BF16)] * (nw if cast else 0),
        compiler_params=_cparams(2, vmem // MIB + 6),
        name=name,
    )(*args)


def _mmk_kernel(lhs_ref, w_ref, res_ref, out_ref, acc_ref):
    k = pl.program_id(2)

    @pl.when(k == 0)
    def _():
        acc_ref[...] = jnp.zeros_like(acc_ref)

    acc_ref[...] += jnp.dot(lhs_ref[...], w_ref[...], preferred_element_type=F32)

    @pl.when(k == pl.num_programs(2) - 1)
    def _():
        out_ref[...] = res_ref[...] + acc_ref[...]


def matmul_ktiled_residual(lhs, w, res, *, tm=1024, tn=1024, tk=1024, name="matmul_k"):
    M, K = lhs.shape
    N = w.shape[1]
    tm, tn, tk = min(tm, M), min(tn, N), min(tk, K)
    assert M % tm == 0 and N % tn == 0 and K % tk == 0
    vmem = 2 * tm * tk * 2 + 2 * tk * tn * 2 + 5 * tm * tn * 4
    return pl.pallas_call(
        _mmk_kernel,
        grid=(M // tm, N // tn, K // tk),
        in_specs=[pl.BlockSpec((tm, tk), lambda m, n, k: (m, k)),
                  pl.BlockSpec((tk, tn), lambda m, n, k: (k, n)),
                  pl.BlockSpec((tm, tn), lambda m, n, k: (m, n))],
        out_specs=pl.BlockSpec((tm, tn), lambda m, n, k: (m, n)),
        out_shape=jax.ShapeDtypeStruct((M, N), F32),
        scratch_shapes=[pltpu.VMEM((tm, tn), F32)],
        compiler_params=_cparams(3, vmem // MIB + 6),
        name=name,
    )(lhs, w, res)


def _t5_bucket_np(dist, n_buckets, max_dist):
    max_exact = n_buckets // 2
    d = np.maximum(dist, 1).astype(np.float32)
    large = max_exact + (np.log(d / np.float32(max_exact)) / np.float32(np.log(max_dist / max_exact))
                         * np.float32(n_buckets - max_exact)).astype(np.int32)
    large = np.minimum(large, n_buckets - 1)
    return np.where(dist < max_exact, dist, large).astype(np.int32)


def _band_kernel(tab_ref, bkt_ref, q_ref, kp_ref, kc_ref, vp_ref, vc_ref, qn_ref, kn_ref,
                 o_ref, lse_ref, bias_ref, *, hpg, hd, head0, buckets, scale):
    p = pl.program_id(0)
    n = pl.program_id(1)
    bb = q_ref.shape[0]

    @pl.when((p == 0) & (n == 0))
    def _():
        bkt = bkt_ref[...]
        for h in range(hpg):
            acc = jnp.full((bb, 2 * bb), NEG, F32)
            for b in buckets:
                acc = jnp.where(bkt == b, tab_ref[b, head0 + h], acc)
            bias_ref[h] = acc

    ki = lax.broadcasted_iota(I32, (bb, 2 * bb), 1)
    kvalid = (ki >= bb) | (n > 0)
    for h in range(hpg):
        sl = slice(h * hd, (h + 1) * hd)
        q = _rms(q_ref[:, sl].astype(F32), qn_ref[...]) * scale
        k = jnp.concatenate([kp_ref[:, sl], kc_ref[:, sl]], axis=0).astype(F32)
        k = _rms(k, kn_ref[...])
        v = jnp.concatenate([vp_ref[:, sl], vc_ref[:, sl]], axis=0)
        s = lax.dot_general(q.astype(BF16), k.astype(BF16), (((1,), (1,)), ((), ())),
                            preferred_element_type=F32)
        s = jnp.where(kvalid, s + bias_ref[h], NEG)
        m = jnp.max(s, axis=-1, keepdims=True)
        e = jnp.exp(s - m)
        l = jnp.sum(e, axis=-1, keepdims=True)
        o = jnp.dot((e / l).astype(BF16), v, preferred_element_type=F32)
        o_ref[:, sl] = o.astype(o_ref.dtype)
        lse_ref[:, sl] = jnp.broadcast_to(m + jnp.log(l), (bb, hd))


def band_attention(proj_g, rel_bias, qn3, kn3, l, g, cfg):
    S, npa = proj_g.shape
    win, dil = cfg.a_groups[g]
    steps = win // dil
    bb, hpg, hd = cfg.band_block, cfg.heads_per_group, cfg.hd_a
    gw = hpg * hd
    ls = S // dil
    assert S % dil == 0 and ls % bb == 0 and steps <= bb and npa == 3 * gw
    nblk = ls // bb
    nb = 3
    x = proj_g.reshape(ls, dil * npa)

    qi = np.arange(bb)[:, None]
    ki = np.arange(2 * bb)[None, :]
    rel = qi + bb - ki
    inside = (rel >= 0) & (rel <= steps)
    bkt = np.where(inside, _t5_bucket_np(np.maximum(rel, 0) * dil, cfg.n_buckets, cfg.max_dist), -1)
    buckets = tuple(int(b) for b in np.unique(bkt[inside]))

    def qmap(p, n):
        return (n, p * nb)

    def kmap_c(p, n):
        return (n, p * nb + 1)

    def kmap_p(p, n):
        return (jnp.maximum(n - 1, 0), p * nb + 1)

    def vmap_c(p, n):
        return (n, p * nb + 2)

    def vmap_p(p, n):
        return (jnp.maximum(n - 1, 0), p * nb + 2)

    blk = (bb, gw)
    o, lse = pl.pallas_call(
        functools.partial(_band_kernel, hpg=hpg, hd=hd, head0=g * hpg, buckets=buckets,
                          scale=float(hd) ** -0.5),
        grid=(dil, nblk),
        in_specs=[pl.BlockSpec(memory_space=pltpu.SMEM),
                  pl.BlockSpec((bb, 2 * bb), lambda p, n: (0, 0)),
                  pl.BlockSpec(blk, qmap), pl.BlockSpec(blk, kmap_p), pl.BlockSpec(blk, kmap_c),
                  pl.BlockSpec(blk, vmap_p), pl.BlockSpec(blk, vmap_c),
                  pl.BlockSpec((None, 1, hd), lambda p, n: (l, 0, 0)),
                  pl.BlockSpec((None, 1, hd), lambda p, n: (l, 0, 0))],
        out_specs=[pl.BlockSpec(blk, lambda p, n: (n, p)), pl.BlockSpec(blk, lambda p, n: (n, p))],
        out_shape=[jax.ShapeDtypeStruct((ls, dil * gw), BF16),
                   jax.ShapeDtypeStruct((ls, dil * gw), F32)],
        scratch_shapes=[pltpu.VMEM((hpg, bb, 2 * bb), F32)],
        compiler_params=_cparams(2, 32),
        name=f"band_attn_g{g}",
    )(rel_bias, jnp.asarray(bkt, I32), x, x, x, x, x, qn3, kn3)
    return o.reshape(S, gw), lse.reshape(S, gw)


def _alpha_kernel(*refs, ng, gw):
    o_refs, l_refs, y_ref = refs[:ng], refs[ng:2 * ng], refs[2 * ng]
    ls = [r[...] for r in l_refs]
    m = functools.reduce(jnp.maximum, ls)
    es = [jnp.exp(v - m) for v in ls]
    den = functools.reduce(lambda a, b: a + b, es)
    for g in range(ng):
        y_ref[:, g * gw:(g + 1) * gw] = (es[g] / den * o_refs[g][...].astype(F32)).astype(y_ref.dtype)


def alpha_merge(outs, lses, tm=512):
    ng = len(outs)
    S, gw = outs[0].shape
    tm = min(tm, S)
    spec = pl.BlockSpec((tm, gw), lambda i: (i, 0))
    return pl.pallas_call(
        functools.partial(_alpha_kernel, ng=ng, gw=gw),
        grid=(S // tm,),
        in_specs=[spec] * (2 * ng),
        out_specs=pl.BlockSpec((tm, ng * gw), lambda i: (i, 0)),
        out_shape=jax.ShapeDtypeStruct((S, ng * gw), BF16),
        compiler_params=_cparams(1, 32),
        name="alpha_merge",
    )(*outs, *lses)


def _cross_kernel(q_ref, k_ref, v_ref, qn_ref, kn_ref, o_ref, *, scale):
    q = _rms(q_ref[...].astype(F32), qn_ref[...]) * scale
    k = _rms(k_ref[...].astype(F32), kn_ref[...])
    s = lax.dot_general(q.astype(BF16), k.astype(BF16), (((1,), (1,)), ((), ())),
                        preferred_element_type=F32)
    m = jnp.max(s, axis=-1, keepdims=True)
    e = jnp.exp(s - m)
    pr = e / jnp.sum(e, axis=-1, keepdims=True)
    o_ref[...] = jnp.dot(pr.astype(BF16), v_ref[...], preferred_element_type=F32).astype(o_ref.dtype)


def cross_attention(proj_t, q_col0, kv, qn3, kn3, l, cfg, tm=1024):
    S = proj_t.shape[0]
    mlen = kv.shape[0]
    hm, hd = cfg.h_m, cfg.hd_m
    tm = min(tm, S)
    assert q_col0 % hd == 0
    qoff = q_col0 // hd
    return pl.pallas_call(
        functools.partial(_cross_kernel, scale=float(hd) ** -0.5),
        grid=(S // tm, hm),
        in_specs=[pl.BlockSpec((tm, hd), lambda i, h: (i, qoff + h)),
                  pl.BlockSpec((mlen, hd), lambda i, h: (0, h)),
                  pl.BlockSpec((mlen, hd), lambda i, h: (0, hm + h)),
                  pl.BlockSpec((None, 1, hd), lambda i, h: (l, 0, 0)),
                  pl.BlockSpec((None, 1, hd), lambda i, h: (l, 0, 0))],
        out_specs=pl.BlockSpec((tm, hd), lambda i, h: (i, h)),
        out_shape=jax.ShapeDtypeStruct((S, hm * hd), BF16),
        compiler_params=_cparams(2, 32),
        name="cross_attn",
    )(proj_t, kv, kv, qn3, kn3)


def _shift_rows(x, tail, s):
    rolled = pltpu.roll(x, s, axis=0)
    row = lax.broadcasted_iota(I32, (SUBLANES, x.shape[1]), 0)
    head = jnp.where(row >= s, rolled[:SUBLANES], pltpu.roll(tail, s, axis=0))
    return jnp.concatenate([head, rolled[SUBLANES:]], axis=0)


def _mlstm_kernel(q_ref, k_ref, v_ref, g_ref, ob_ref, cw_ref, cb_ref, gb_ref, hn_ref,
                  y_ref, cn_ref, ms_ref, tail_ref, *, nh, hd, dp, conv_w):
    c = pl.program_id(0)
    L = q_ref.shape[0]
    wb = nh * hd

    @pl.when(c == 0)
    def _():
        cn_ref[...] = jnp.zeros_like(cn_ref)
        ms_ref[...] = jnp.zeros_like(ms_ref)
        tail_ref[...] = jnp.zeros_like(tail_ref)

    def conv_silu(x_ref, col0):
        x = x_ref[...].astype(F32)
        tail = tail_ref[:, col0:col0 + wb]
        y = cb_ref[:, col0:col0 + wb] + cw_ref[conv_w - 1:conv_w, col0:col0 + wb] * x
        for s in range(1, conv_w):
            y = y + cw_ref[conv_w - 1 - s:conv_w - s, col0:col0 + wb] * _shift_rows(x, tail, s)
        tail_ref[:, col0:col0 + wb] = x[L - SUBLANES:]
        return y * _sigmoid(y)

    qs = conv_silu(q_ref, 0)
    ks = conv_silu(k_ref, wb) * (float(hd) ** -0.5)

    G = g_ref[...] + gb_ref[...]
    lf = jnp.minimum(G, 0.0) - jnp.log(1.0 + jnp.exp(-jnp.abs(G)))
    row = lax.broadcasted_iota(I32, (L, LANES), 0)
    F = lf
    sh = 1
    while sh < L:
        F = F + jnp.where(row >= sh, pltpu.roll(F, sh, axis=0), 0.0)
        sh *= 2
    GT = G.T
    FT = F.T
    ti = lax.broadcasted_iota(I32, (L, L), 0)
    si = lax.broadcasted_iota(I32, (L, L), 1)
    causal = ti >= si
    ones_col = (lax.broadcasted_iota(I32, (L, dp - hd), 1) == 0).astype(BF16)
    ms = ms_ref[...]

    for h in range(nh):
        sl = slice(h * hd, (h + 1) * hd)
        li_c, F_c = G[:, h:h + 1], F[:, nh + h:nh + h + 1]
        li_r, F_r = GT[h:h + 1, :], FT[nh + h:nh + h + 1, :]
        F_last = F_c[L - 1:L, :]
        m_prev = ms[:, h:h + 1]
        a_r = F_last - F_r + li_r
        b = jnp.max(a_r, axis=-1, keepdims=True)
        ea_c = jnp.exp(F_last - F_c + li_c - b)
        logw = jnp.where(causal, F_c - F_r + li_r, NEG)
        m_intra = jnp.max(logw, axis=-1, keepdims=True)
        m_inter = F_c + m_prev
        m_t = jnp.maximum(m_inter, m_intra)
        q = qs[:, sl].astype(BF16)
        k = ks[:, sl]
        v_aug = jnp.concatenate([v_ref[:, sl], ones_col], axis=1)
        s = lax.dot_general(q, k.astype(BF16), (((1,), (1,)), ((), ())),
                            preferred_element_type=F32) * jnp.exp(logw - m_t)
        inter = jnp.exp(m_inter - m_t)
        cn = cn_ref[h]
        num = (jnp.dot(s.astype(BF16), v_aug, preferred_element_type=F32)
               + inter * jnp.dot(q, cn.astype(BF16), preferred_element_type=F32))
        den = num[:, hd:hd + 1]
        hv = num[:, :hd] / jnp.maximum(jnp.abs(den), jnp.exp(-m_t))
        m_new = jnp.maximum(F_last + m_prev, b)
        decay = jnp.exp(F_last + m_prev - m_new)
        inj = jnp.exp(b - m_new)
        kv = jnp.dot((ea_c * k).T.astype(BF16), v_aug, preferred_element_type=F32)
        cn_ref[h] = decay * cn + inj * kv
        ms_ref[:, h:h + 1] = m_new
        hb = _rms(hv, hn_ref[:, sl])
        y_ref[:, sl] = (_sigmoid(ob_ref[:, sl].astype(F32)) * hb).astype(y_ref.dtype)


def mlstm(proj_b, gates, proj_t, conv_w, conv_b3, gbias3, hnorm3, l, cfg):
    S = proj_b.shape[0]
    L, nh, hd = cfg.chunk, cfg.h_b, cfg.hd_b
    wb = nh * hd
    dp = _round_up(hd + 1, LANES)
    assert L == LANES and S % L == 0 and 2 * nh <= LANES and cfg.conv_w <= SUBLANES
    cw = conv_w.shape[1]
    return pl.pallas_call(
        functools.partial(_mlstm_kernel, nh=nh, hd=hd, dp=dp, conv_w=cw),
        grid=(S // L,),
        in_specs=[pl.BlockSpec((L, wb), lambda c: (c, 0)),
                  pl.BlockSpec((L, wb), lambda c: (c, 1)),
                  pl.BlockSpec((L, wb), lambda c: (c, 2)),
                  pl.BlockSpec((L, LANES), lambda c: (c, 0)),
                  pl.BlockSpec((L, wb), lambda c: (c, 0)),
                  pl.BlockSpec((None, cw, 2 * wb), lambda c: (l, 0, 0)),
                  pl.BlockSpec((None, 1, 2 * wb), lambda c: (l, 0, 0)),
                  pl.BlockSpec((None, 1, LANES), lambda c: (l, 0, 0)),
                  pl.BlockSpec((None, 1, wb), lambda c: (l, 0, 0))],
        out_specs=pl.BlockSpec((L, wb), lambda c: (c, 0)),
        out_shape=jax.ShapeDtypeStruct((S, wb), BF16),
        scratch_shapes=[pltpu.VMEM((nh, hd, dp), F32), pltpu.VMEM((1, LANES), F32),
                        pltpu.VMEM((SUBLANES, 2 * wb), F32)],
        compiler_params=_cparams(1, 40),
        name="mlstm",
    )(proj_b, proj_b, proj_b, gates, proj_t, conv_w, conv_b3, gbias3, hnorm3)


def _merge_kernel(ya_ref, yb_ref, ym_ref, wa_ref, wb_ref, wm_ref, ga_ref, gb_ref, gm_ref,
                  o_ref, was, wbs, wms):
    @pl.when(pl.program_id(1) == 0)
    def _():
        was[...] = wa_ref[...].astype(BF16)
        wbs[...] = wb_ref[...].astype(BF16)
        wms[...] = wm_ref[...].astype(BF16)

    def branch(y_ref, w_ref, g_ref):
        return _sigmoid(g_ref[...].astype(F32)) * jnp.dot(y_ref[...], w_ref[...],
                                                          preferred_element_type=F32)

    o_ref[...] = (branch(ya_ref, was, ga_ref) + branch(yb_ref, wbs, gb_ref)
                  + branch(ym_ref, wms, gm_ref)).astype(o_ref.dtype)


def gated_merge(ya, yb, ym, w_a, w_b, w_m, proj_t, gate_col0, l, d, tm=1024, tn=512):
    S = ya.shape[0]
    tm = min(tm, S)
    while gate_col0 % tn or d % tn:
        tn //= 2
    goff = gate_col0 // tn
    nd = d // tn

    def lhs_spec(y):
        return pl.BlockSpec((tm, y.shape[1]), lambda n, m: (m, 0))

    def w_spec(w):
        return pl.BlockSpec((None, w.shape[1], tn), lambda n, m: (l, 0, n))

    def g_spec(j):
        return pl.BlockSpec((tm, tn), lambda n, m: (m, goff + j * nd + n))

    ksum = ya.shape[1] + yb.shape[1] + ym.shape[1]
    vmem = 2 * tm * ksum * 2 + 2 * ksum * tn * 4 + ksum * tn * 2 + 8 * tm * tn * 2 + 4 * tm * tn * 4
    return pl.pallas_call(
        _merge_kernel,
        grid=(nd, S // tm),
        in_specs=[lhs_spec(ya), lhs_spec(yb), lhs_spec(ym), w_spec(w_a), w_spec(w_b), w_spec(w_m),
                  g_spec(0), g_spec(1), g_spec(2)],
        out_specs=pl.BlockSpec((tm, tn), lambda n, m: (m, n)),
        out_shape=jax.ShapeDtypeStruct((S, d), BF16),
        scratch_shapes=[pltpu.VMEM((w.shape[1], tn), BF16) for w in (w_a, w_b, w_m)],
        compiler_params=_cparams(2, vmem // MIB + 6),
        name="gated_merge",
    )(ya, yb, ym, w_a, w_b, w_m, proj_t, proj_t, proj_t)


def _router_kernel(x_ref, g_ref, wr_ref, br_ref, h_ref, meta_ref, gate_ref, cnt_ref, run_ref, *, n_exp):
    i = pl.program_id(0)
    tm = x_ref.shape[0]

    @pl.when(i == 0)
    def _():
        run_ref[...] = jnp.zeros_like(run_ref)

    y = _rms(x_ref[...], g_ref[...])
    h_ref[...] = y
    logits = jnp.dot(y, wr_ref[...], preferred_element_type=F32,
                     precision=lax.Precision.HIGHEST) + br_ref[...]
    lane = lax.broadcasted_iota(I32, (tm, LANES), 1)
    lanef = lane.astype(F32)
    logits = jnp.where(lane < n_exp, logits, -jnp.inf)
    v1 = jnp.max(logits, axis=-1, keepdims=True)
    i1 = jnp.min(jnp.where(logits == v1, lanef, float(LANES)), axis=-1, keepdims=True).astype(I32)
    rest = jnp.where(lane == i1, -jnp.inf, logits)
    v2 = jnp.max(rest, axis=-1, keepdims=True)
    i2 = jnp.min(jnp.where(rest == v2, lanef, float(LANES)), axis=-1, keepdims=True).astype(I32)
    e = jnp.exp(v2 - v1)
    g1 = 1.0 / (1.0 + e)
    g2 = e / (1.0 + e)
    oh1 = lane == i1
    oh2 = lane == i2
    oh = jnp.where(oh1 | oh2, 1.0, 0.0)
    r = lax.broadcasted_iota(I32, (tm, tm), 0)
    cidx = lax.broadcasted_iota(I32, (tm, tm), 1)
    tri = jnp.where(cidx < r, 1.0, 0.0).astype(BF16)
    cum = jnp.dot(tri, oh.astype(BF16), preferred_element_type=F32) + run_ref[...]
    r1 = jnp.sum(jnp.where(oh1, cum, 0.0), axis=-1, keepdims=True).astype(I32)
    r2 = jnp.sum(jnp.where(oh2, cum, 0.0), axis=-1, keepdims=True).astype(I32)
    run_ref[...] += jnp.sum(oh, axis=0, keepdims=True)
    meta_ref[...] = jnp.where(lane == 0, i1, jnp.where(lane == 1, i2,
                              jnp.where(lane == 2, r1, jnp.where(lane == 3, r2, 0))))
    gate_ref[...] = jnp.where(lane == 0, g1, jnp.where(lane == 1, g2, 0.0))
    cnt_ref[...] = run_ref[...]


def norm_router(x, g3, w_router_p, b_router_p, l, lr, n_exp, tm=256):
    S, D = x.shape
    tm = min(tm, S)
    return pl.pallas_call(
        functools.partial(_router_kernel, n_exp=n_exp),
        grid=(S // tm,),
        in_specs=[pl.BlockSpec((tm, D), lambda i: (i, 0)),
                  pl.BlockSpec((None, 1, D), lambda i: (l, 0, 0)),
                  pl.BlockSpec((None, D, LANES), lambda i: (lr, 0, 0)),
                  pl.BlockSpec((None, 1, LANES), lambda i: (lr, 0, 0))],
        out_specs=[pl.BlockSpec((tm, D), lambda i: (i, 0)),
                   pl.BlockSpec((tm, LANES), lambda i: (i, 0)),
                   pl.BlockSpec((tm, LANES), lambda i: (i, 0)),
                   pl.BlockSpec((1, LANES), lambda i: (0, 0))],
        out_shape=[jax.ShapeDtypeStruct((S, D), F32), jax.ShapeDtypeStruct((S, LANES), I32),
                   jax.ShapeDtypeStruct((S, LANES), F32), jax.ShapeDtypeStruct((1, LANES), F32)],
        scratch_shapes=[pltpu.VMEM((1, LANES), F32)],
        compiler_params=_cparams(1, 40),
        name="norm_router",
    )(x, g3, w_router_p, b_router_p)


def _row_copy(src_hbm, dst, src_row, dst_row, sem):
    return pltpu.make_async_copy(src_hbm.at[pl.ds(src_row, 1)], dst.at[pl.ds(dst_row, 1)], sem)


def _gather_kernel(order_ref, nv_ref, h_hbm, xs_ref, buf, sem, *, tg):
    i = pl.program_id(0)
    nv = nv_ref[0]
    slot = i % 2

    def start_tile(t, s):
        def body(r, carry):
            _row_copy(h_hbm, buf.at[s], order_ref[t * tg + r], r, sem.at[s]).start()
            return carry
        lax.fori_loop(0, tg, body, 0, unroll=8)

    def wait_tile(s):
        def body(r, carry):
            _row_copy(h_hbm, buf.at[s], 0, r, sem.at[s]).wait()
            return carry
        lax.fori_loop(0, tg, body, 0, unroll=8)

    @pl.when(i == 0)
    def _():
        start_tile(0, 0)

    @pl.when(i + 1 < nv)
    def _():
        start_tile(i + 1, 1 - slot)

    @pl.when(i < nv)
    def _():
        wait_tile(slot)
        xs_ref[...] = buf[slot].astype(BF16)

    @pl.when(i >= nv)
    def _():
        xs_ref[...] = jnp.zeros_like(xs_ref)


def gather_rows(h, order, nvalid, n_tiles, tg):
    D = h.shape[1]
    grid_spec = pltpu.PrefetchScalarGridSpec(
        num_scalar_prefetch=2,
        grid=(n_tiles,),
        in_specs=[pl.BlockSpec(memory_space=pl.ANY)],
        out_specs=pl.BlockSpec((tg, D), lambda i, order, nv: (i, 0)),
        scratch_shapes=[pltpu.VMEM((2, tg, D), F32), pltpu.SemaphoreType.DMA((2,))],
    )
    return pl.pallas_call(
        functools.partial(_gather_kernel, tg=tg),
        grid_spec=grid_spec,
        out_shape=jax.ShapeDtypeStruct((n_tiles * tg, D), BF16),
        compiler_params=_cparams(1, 40),
        name="moe_gather",
    )(order, nvalid, h)


def _gmm_kernel(te_ref, src_ref, first_ref, nv_ref, *refs, nw, epi):
    x_ref = refs[0]
    w_refs = refs[1:1 + nw]
    out_ref = refs[1 + nw]
    ws_refs = refs[2 + nw:2 + 2 * nw]
    i = pl.program_id(1)

    @pl.when(first_ref[i] == 1)
    def _():
        for w_ref, ws_ref in zip(w_refs, ws_refs):
            ws_ref[...] = w_ref[...].astype(BF16)

    @pl.when(i < nv_ref[0])
    def _():
        x = x_ref[...]
        accs = [jnp.dot(x, ws_ref[...], preferred_element_type=F32) for ws_ref in ws_refs]
        if epi == "swiglu":
            g, u = accs
            out = g * _sigmoid(g) * u
        else:
            out = accs[0]
        out_ref[...] = out.astype(out_ref.dtype)

    @pl.when(i >= nv_ref[0])
    def _():
        out_ref[...] = jnp.zeros_like(out_ref)


def grouped_matmul(xs, ws, lr, tile_meta, *, epi, out_dtype, tg, tn, name):
    te, src, first, nvalid = tile_meta
    P, K = xs.shape
    N = ws[0].shape[-1]
    nw = len(ws)
    tn = min(tn, N)
    assert N % tn == 0 and P % tg == 0
    w_spec = pl.BlockSpec((None, None, K, tn), lambda n, i, te, src, first, nv: (lr, te[i], 0, n))
    grid_spec = pltpu.PrefetchScalarGridSpec(
        num_scalar_prefetch=4,
        grid=(N // tn, P // tg),
        in_specs=[pl.BlockSpec((tg, K), lambda n, i, te, src, first, nv: (src[i], 0))] + [w_spec] * nw,
        out_specs=pl.BlockSpec((tg, tn), lambda n, i, te, src, first, nv: (i, n)),
        scratch_shapes=[pltpu.VMEM((K, tn), BF16)] * nw,
    )
    obytes = jnp.dtype(out_dtype).itemsize
    vmem = 2 * tg * K * 2 + nw * (2 * K * tn * 4 + K * tn * 2) + 2 * tg * tn * obytes + (nw + 1) * tg * tn * 4
    return pl.pallas_call(
        functools.partial(_gmm_kernel, nw=nw, epi=epi),
        grid_spec=grid_spec,
        out_shape=jax.ShapeDtypeStruct((P, N), out_dtype),
        compiler_params=_cparams(2, vmem // MIB + 6),
        name=name,
    )(te, src, first, nvalid, xs, *ws)


def _combine_kernel(dest_ref, x_ref, gate_ref, ys_hbm, out_ref, buf, sem, *, tc, top_k):
    i = pl.program_id(0)
    slot = i % 2

    def start_tile(t, s):
        def body(r, carry):
            for k in range(top_k):
                _row_copy(ys_hbm, buf.at[s, k], dest_ref[(t * tc + r) * top_k + k], r, sem.at[s]).start()
            return carry
        lax.fori_loop(0, tc, body, 0, unroll=4)

    def wait_tile(s):
        def body(r, carry):
            for k in range(top_k):
                _row_copy(ys_hbm, buf.at[s, k], 0, r, sem.at[s]).wait()
            return carry
        lax.fori_loop(0, tc, body, 0, unroll=4)

    @pl.when(i == 0)
    def _():
        start_tile(0, 0)

    @pl.when(i + 1 < pl.num_programs(0))
    def _():
        start_tile(i + 1, 1 - slot)

    wait_tile(slot)
    g = gate_ref[...]
    out = x_ref[...]
    for k in range(top_k):
        out = out + g[:, k:k + 1] * buf[slot, k]
    out_ref[...] = out


def moe_combine(x, gates, ys, dest_flat, top_k, tc=256):
    S, D = x.shape
    tc = min(tc, S)
    grid_spec = pltpu.PrefetchScalarGridSpec(
        num_scalar_prefetch=1,
        grid=(S // tc,),
        in_specs=[pl.BlockSpec((tc, D), lambda i, d: (i, 0)),
                  pl.BlockSpec((tc, LANES), lambda i, d: (i, 0)),
                  pl.BlockSpec(memory_space=pl.ANY)],
        out_specs=pl.BlockSpec((tc, D), lambda i, d: (i, 0)),
        scratch_shapes=[pltpu.VMEM((2, top_k, tc, D), F32), pltpu.SemaphoreType.DMA((2,))],
    )
    return pl.pallas_call(
        functools.partial(_combine_kernel, tc=tc, top_k=top_k),
        grid_spec=grid_spec,
        out_shape=jax.ShapeDtypeStruct((S, D), F32),
        compiler_params=_cparams(1, 48),
        name="moe_combine",
    )(dest_flat, x, gates, ys)


def moe_layer(x, norm_ffn3, l, w_router, b_router, w_e_gate, w_e_up, w_e_down, lr, cfg):
    S, D = x.shape
    E, top_k, tg = cfg.n_experts, 2, cfg.tg
    tg = min(tg, S)
    wr = jnp.pad(w_router, ((0, 0), (0, 0), (0, LANES - E)))
    br = jnp.pad(b_router, ((0, 0), (0, LANES - E)))[:, None, :]
    h, meta, gates, cnt = norm_router(x, norm_ffn3, wr, br, l, lr, E)

    eid, rank = meta[:, 0:top_k], meta[:, top_k:2 * top_k]
    counts = cnt[0, :E].astype(I32)
    padded = (counts + tg - 1) // tg * tg
    ends = jnp.cumsum(padded)
    dest = (ends - padded)[eid] + rank
    n_tiles = (S * top_k) // tg + E
    order = jnp.zeros((n_tiles * tg,), I32).at[dest.reshape(-1)].set(
        jnp.repeat(jnp.arange(S, dtype=I32), top_k))
    nvalid = (ends[-1] // tg).astype(I32)
    tile = jnp.arange(n_tiles, dtype=I32)
    src = jnp.minimum(tile, nvalid - 1)
    te = jnp.sum((src * tg)[:, None] >= ends[None, :], axis=1).astype(I32)
    first = ((tile == 0) | (te != jnp.roll(te, 1))).astype(I32)
    nv = nvalid.reshape(1)
    tile_meta = (te, src, first, nv)

    xs = gather_rows(h, order, nv, n_tiles, tg)
    a = grouped_matmul(xs, [w_e_gate, w_e_up], lr, tile_meta, epi="swiglu", out_dtype=BF16,
                       tg=tg, tn=cfg.tn, name="moe_gate_up")
    ys = grouped_matmul(a, [w_e_down], lr, tile_meta, epi="plain", out_dtype=F32,
                        tg=tg, tn=cfg.tn, name="moe_down")
    return moe_combine(x, gates, ys, dest.reshape(-1), top_k)


def _forward(x, mem, rel_bias, norm_mix, norm_ffn, norm_mem, w_in, qn_a, kn_a, conv_w, conv_b,
             gate_bias_b, hnorm_b, w_mem_kv, qn_m, kn_m, w_br_a, w_br_b, w_br_m, w_out,
             w_ff_gate, w_ff_up, w_ff_down, w_router, b_router, w_e_gate, w_e_up, w_e_down, cfg):
    B, S, D = x.shape
    assert B == 1 and mem.shape[0] == 1
    depth = norm_mix.shape[0]
    ng = len(cfg.a_groups)
    w_a = ng * cfg.heads_per_group * cfg.hd_a
    w_b = cfg.h_b * cfg.hd_b
    w_m = cfg.h_m * cfg.hd_m
    if_col0 = 3 * w_a + 3 * w_b
    tail_col0 = if_col0 + 2 * cfg.h_b
    n_tail = w_b + w_m + 3 * D
    d_ff = w_ff_gate.shape[-1]
    d_ff_p = _round_up(d_ff, cfg.tk_down) if d_ff > cfg.tk_down else _round_up(d_ff, LANES)

    def row3(p):
        return p[:, None, :]

    x = x.reshape(S, D)
    mem2 = mem.reshape(mem.shape[1], D)
    norm_mix3, norm_ffn3, norm_mem3 = row3(norm_mix), row3(norm_ffn), row3(norm_mem)
    qn_a3, kn_a3, qn_m3, kn_m3 = row3(qn_a), row3(kn_a), row3(qn_m), row3(kn_m)
    conv_b3, hnorm3 = row3(conv_b), row3(hnorm_b)
    gbias3 = row3(jnp.pad(gate_bias_b, ((0, 0), (0, LANES - 2 * cfg.h_b))))
    mm = functools.partial(matmul, tm=cfg.tm)
    gw = cfg.heads_per_group * cfg.hd_a
    w_in_t = jnp.swapaxes(w_in, 1, 2)

    for l in range(depth):
        h = rmsnorm(x, norm_mix3, l)
        proj_g = [mm(h, [w_in_t], lead=l, trans=True, n_out=3 * gw, tn=gw,
                     blk_of=lambda n, g=g: n * ng + g, name=f"proj_a{g}")
                  for g in range(ng)]
        proj_b = mm(h, [w_in_t], lead=l, trans=True, col0=3 * w_a, n_out=3 * w_b, tn=cfg.tn,
                    name="proj_b")
        gates_b = mm(h, [w_in_t], lead=l, trans=True, col0=if_col0, n_out=LANES, out_dtype=F32,
                     tn=LANES, name="proj_if")
        proj_t = mm(h, [w_in_t], lead=l, trans=True, col0=tail_col0, n_out=n_tail, tn=cfg.tn,
                    name="proj_tail")

        outs, lses = zip(*[band_attention(proj_g[g], rel_bias, qn_a3, kn_a3, l, g, cfg)
                           for g in range(ng)])
        y_a = alpha_merge(outs, lses)
        y_b = mlstm(proj_b, gates_b, proj_t, conv_w, conv_b3, gbias3, hnorm3, l, cfg)
        hm = rmsnorm(mem2, norm_mem3, l)
        kv = mm(hm, [w_mem_kv], lead=l, n_out=2 * w_m, tn=cfg.tn, name="mem_kv")
        y_m = cross_attention(proj_t, w_b, kv, qn_m3, kn_m3, l, cfg)
        y = gated_merge(y_a, y_b, y_m, w_br_a, w_br_b, w_br_m, proj_t, w_b + w_m, l, D,
                        tm=cfg.tm // 2, tn=cfg.tn)
        x = mm(y, [w_out], lead=l, n_out=D, epi="residual", res=x, out_dtype=F32, tn=cfg.tn,
               name="out_proj")

        if l % 2 == 0:
            ld = l // 2
            h2 = rmsnorm(x, norm_ffn3, l)
            a = mm(h2, [w_ff_gate, w_ff_up], lead=ld, n_out=d_ff_p, epi="swiglu", n_valid=d_ff,
                   tn=cfg.tn2, name="ffn_gate_up")
            wd = jnp.pad(w_ff_down[ld], ((0, d_ff_p - d_ff), (0, 0))).astype(BF16)
            x = matmul_ktiled_residual(a, wd, x, tm=cfg.tm, tn=1024, tk=cfg.tk_down, name="ffn_down")
        else:
            x = moe_layer(x, norm_ffn3, l, w_router, b_router, w_e_gate, w_e_up, w_e_down, l // 2, cfg)
    return x.reshape(B, S, D)


def kernel(x, mem, rel_bias, norm_mix, norm_ffn, norm_mem, w_in, qn_a, kn_a, conv_w, conv_b, gate_bias_b, hnorm_b, w_mem_kv, qn_m, kn_m, w_br_a, w_br_b, w_br_m, w_out, w_ff_gate, w_ff_up, w_ff_down, w_router, b_router, w_e_gate, w_e_up, w_e_down):
    return _forward(x, mem, rel_bias, norm_mix, norm_ffn, norm_mem, w_in, qn_a, kn_a, conv_w, conv_b,
                    gate_bias_b, hnorm_b, w_mem_kv, qn_m, kn_m, w_br_a, w_br_b, w_br_m, w_out,
                    w_ff_gate, w_ff_up, w_ff_down, w_router, b_router, w_e_gate, w_e_up, w_e_down,
                    Cfg())
```

```python
import functools
from typing import NamedTuple

import numpy as np
import jax
import jax.numpy as jnp
from jax import lax
from jax.experimental import pallas as pl
from jax.experimental.pallas import tpu as pltpu

F32 = jnp.float32
BF16 = jnp.bfloat16
I32 = jnp.int32
EPS = 1e-6
NEG = -1e30
MIB = 1 << 20
LANES = 128
SUBLANES = 8
VMEM_CAP_MIB = 60


class Cfg(NamedTuple):
    a_groups: tuple = ((128, 1), (512, 4), (2048, 16))
    heads_per_group: int = 4
    hd_a: int = 128
    band_block: int = 128
    h_b: int = 4
    hd_b: int = 384
    chunk: int = 128
    conv_w: int = 4
    h_m: int = 4
    hd_m: int = 256
    n_buckets: int = 32
    max_dist: int = 2048
    n_experts: int = 8
    tm: int = 512
    tn: int = 1024
    tm_b: int = 1024
    tn_b: int = 512
    tn2: int = 512
    tn_gd: int = 1024
    tg: int = 512
    tk_down: int = 2816


def _cparams(n_axes, vmem_mib):
    return pltpu.CompilerParams(dimension_semantics=("arbitrary",) * n_axes,
                                vmem_limit_bytes=int(min(vmem_mib, VMEM_CAP_MIB)) * MIB)


def _round_up(a, b):
    return -(-a // b) * b


def _sigmoid(x):
    return 1.0 / (1.0 + jnp.exp(-x))


def _rms(x, g):
    return x * lax.rsqrt(jnp.mean(x * x, axis=-1, keepdims=True) + EPS) * g


def _rmsnorm_kernel(x_ref, g_ref, o_ref):
    o_ref[...] = _rms(x_ref[...].astype(F32), g_ref[...]).astype(o_ref.dtype)


def rmsnorm(x, g3, l, out_dtype=BF16, tm=256):
    M, D = x.shape
    tm = min(tm, M)
    return pl.pallas_call(
        _rmsnorm_kernel,
        grid=(M // tm,),
        in_specs=[pl.BlockSpec((tm, D), lambda i: (i, 0)),
                  pl.BlockSpec((None, 1, D), lambda i: (l, 0, 0))],
        out_specs=pl.BlockSpec((tm, D), lambda i: (i, 0)),
        out_shape=jax.ShapeDtypeStruct((M, D), out_dtype),
        compiler_params=_cparams(1, 32),
        name="rmsnorm",
    )(x, g3)


def _mm_kernel(*refs, nw, cast, trans, epi, n_valid, tn):
    lhs_ref = refs[0]
    w_refs = refs[1:1 + nw]
    pos = 1 + nw
    res_ref = None
    if epi == "residual":
        res_ref = refs[pos]
        pos += 1
    out_ref = refs[pos]
    pos += 1
    ws_refs = refs[pos:pos + nw] if cast else w_refs

    if cast:
        @pl.when(pl.program_id(1) == 0)
        def _():
            for w_ref, ws_ref in zip(w_refs, ws_refs):
                ws_ref[...] = w_ref[...].astype(BF16)

    lhs = lhs_ref[...]
    dims = (((1,), (1,)), ((), ())) if trans else (((1,), (0,)), ((), ()))
    accs = [lax.dot_general(lhs, ws_ref[...], dims, preferred_element_type=F32) for ws_ref in ws_refs]
    if epi == "plain":
        out = accs[0]
    elif epi == "residual":
        out = res_ref[...] + accs[0]
    else:
        g, u = accs
        out = g * _sigmoid(g) * u
        if n_valid is not None:
            col = pl.program_id(0) * tn + lax.broadcasted_iota(I32, out.shape, 1)
            out = jnp.where(col < n_valid, out, 0.0)
    out_ref[...] = out.astype(out_ref.dtype)


def matmul(lhs, ws, *, lead=None, col0=0, n_out, epi="plain", res=None, n_valid=None,
           trans=False, blk_of=None, out_dtype=BF16, tm=1024, tn=512, name="matmul"):
    M, K = lhs.shape
    nw = len(ws)
    cast = ws[0].dtype != BF16
    tm = min(tm, M)
    tn = min(tn, n_out)
    if not trans:
        while col0 % tn:
            tn //= 2
    assert tn >= LANES and M % tm == 0
    gn = pl.cdiv(n_out, tn)
    if trans and col0 % tn == 0:
        assert lead is not None and n_out % tn == 0
        if blk_of is None:
            def blk_of(n):
                return col0 // tn + n
        w_spec = pl.BlockSpec((None, tn, K), lambda n, m: (lead, blk_of(n), 0))
        w_block = (tn, K)
    elif trans:
        n_rows = ws[0].shape[1]
        assert lead is not None and n_out % tn == 0 and blk_of is None
        assert n_rows % SUBLANES == 0 and col0 % SUBLANES == 0 and tn % SUBLANES == 0
        row8 = (lead * n_rows + col0) // SUBLANES
        ws = [w.reshape(-1, K) for w in ws]
        w_spec = pl.BlockSpec((pl.Element(tn), pl.Element(K)),
                              lambda n, m: ((row8 + n * (tn // SUBLANES)) * SUBLANES, 0))
        w_block = (tn, K)
    else:
        off = col0 // tn
        wlast = pl.cdiv(ws[0].shape[-1], tn) - 1
        if lead is None:
            w_spec = pl.BlockSpec((K, tn), lambda n, m: (0, jnp.minimum(n + off, wlast)))
        else:
            w_spec = pl.BlockSpec((None, K, tn), lambda n, m: (lead, 0, jnp.minimum(n + off, wlast)))
        w_block = (K, tn)
    in_specs = [pl.BlockSpec((tm, K), lambda n, m: (m, 0))] + [w_spec] * nw
    args = [lhs] + list(ws)
    if epi == "residual":
        in_specs.append(pl.BlockSpec((tm, tn), lambda n, m: (m, n)))
        args.append(res)
    wbytes = ws[0].dtype.itemsize
    obytes = jnp.dtype(out_dtype).itemsize
    vmem = (2 * tm * K * 2 + nw * 2 * K * tn * wbytes + (nw * K * tn * 2 if cast else 0)
            + 2 * tm * tn * obytes + (2 * tm * tn * 4 if epi == "residual" else 0)
            + (nw + 1) * tm * tn * 4)
    return pl.pallas_call(
        functools.partial(_mm_kernel, nw=nw, cast=cast, trans=trans, epi=epi, n_valid=n_valid, tn=tn),
        grid=(gn, M // tm),
        in_specs=in_specs,
        out_specs=pl.BlockSpec((tm, tn), lambda n, m: (m, n)),
        out_shape=jax.ShapeDtypeStruct((M, n_out), out_dtype),
        scratch_shapes=[pltpu.VMEM(w_block, BF16)] * (nw if cast else 0),
        compiler_params=_cparams(2, vmem // MIB + 6),
        name=name,
    )(*args)


def _mmk_kernel(lhs_ref, w_ref, res_ref, out_ref, acc_ref):
    k = pl.program_id(2)

    @pl.when(k == 0)
    def _():
        acc_ref[...] = jnp.zeros_like(acc_ref)

    acc_ref[...] += jnp.dot(lhs_ref[...], w_ref[...], preferred_element_type=F32)

    @pl.when(k == pl.num_programs(2) - 1)
    def _():
        out_ref[...] = res_ref[...] + acc_ref[...]


def matmul_ktiled_residual(lhs, w, res, *, tm=1024, tn=1024, tk=1024, name="matmul_k"):
    M, K = lhs.shape
    N = w.shape[1]
    tm, tn, tk = min(tm, M), min(tn, N), min(tk, K)
    assert M % tm == 0 and N % tn == 0 and K % tk == 0
    vmem = 2 * tm * tk * 2 + 2 * tk * tn * 2 + 5 * tm * tn * 4
    return pl.pallas_call(
        _mmk_kernel,
        grid=(M // tm, N // tn, K // tk),
        in_specs=[pl.BlockSpec((tm, tk), lambda m, n, k: (m, k)),
                  pl.BlockSpec((tk, tn), lambda m, n, k: (k, n)),
                  pl.BlockSpec((tm, tn), lambda m, n, k: (m, n))],
        out_specs=pl.BlockSpec((tm, tn), lambda m, n, k: (m, n)),
        out_shape=jax.ShapeDtypeStruct((M, N), F32),
        scratch_shapes=[pltpu.VMEM((tm, tn), F32)],
        compiler_params=_cparams(3, vmem // MIB + 6),
        name=name,
    )(lhs, w, res)


def _t5_bucket_np(dist, n_buckets, max_dist):
    max_exact = n_buckets // 2
    d = np.maximum(dist, 1).astype(np.float32)
    large = max_exact + (np.log(d / np.float32(max_exact)) / np.float32(np.log(max_dist / max_exact))
                         * np.float32(n_buckets - max_exact)).astype(np.int32)
    large = np.minimum(large, n_buckets - 1)
    return np.where(dist < max_exact, dist, large).astype(np.int32)


def _band_kernel(tab_ref, bkt_ref, q_ref, kp_ref, kc_ref, vp_ref, vc_ref, qn_ref, kn_ref,
                 o_ref, lse_ref, bias_ref, *, hpg, hd, head0, buckets, scale):
    p = pl.program_id(0)
    n = pl.program_id(1)
    bb = q_ref.shape[0]

    @pl.when((p == 0) & (n == 0))
    def _():
        bkt = bkt_ref[...]
        for h in range(hpg):
            acc = jnp.full((bb, 2 * bb), NEG, F32)
            for b in buckets:
                acc = jnp.where(bkt == b, tab_ref[b, head0 + h], acc)
            bias_ref[h] = acc

    ki = lax.broadcasted_iota(I32, (bb, 2 * bb), 1)
    kvalid = (ki >= bb) | (n > 0)
    for h in range(hpg):
        sl = slice(h * hd, (h + 1) * hd)
        q = _rms(q_ref[:, sl].astype(F32), qn_ref[...]) * scale
        k = jnp.concatenate([kp_ref[:, sl], kc_ref[:, sl]], axis=0).astype(F32)
        k = _rms(k, kn_ref[...])
        v = jnp.concatenate([vp_ref[:, sl], vc_ref[:, sl]], axis=0)
        s = lax.dot_general(q.astype(BF16), k.astype(BF16), (((1,), (1,)), ((), ())),
                            preferred_element_type=F32)
        s = jnp.where(kvalid, s + bias_ref[h], NEG)
        m = jnp.max(s, axis=-1, keepdims=True)
        e = jnp.exp(s - m)
        l = jnp.sum(e, axis=-1, keepdims=True)
        o = jnp.dot((e / l).astype(BF16), v, preferred_element_type=F32)
        o_ref[:, sl] = o.astype(o_ref.dtype)
        lse_ref[:, sl] = jnp.broadcast_to(m + jnp.log(l), (bb, hd))


def band_attention(proj_g, rel_bias, qn3, kn3, l, g, cfg):
    S, npa = proj_g.shape
    win, dil = cfg.a_groups[g]
    steps = win // dil
    bb, hpg, hd = cfg.band_block, cfg.heads_per_group, cfg.hd_a
    gw = hpg * hd
    ls = S // dil
    assert S % dil == 0 and ls % bb == 0 and steps <= bb and npa == 3 * gw
    nblk = ls // bb
    nb = 3
    x = proj_g.reshape(ls, dil * npa)

    qi = np.arange(bb)[:, None]
    ki = np.arange(2 * bb)[None, :]
    rel = qi + bb - ki
    inside = (rel >= 0) & (rel <= steps)
    bkt = np.where(inside, _t5_bucket_np(np.maximum(rel, 0) * dil, cfg.n_buckets, cfg.max_dist), -1)
    buckets = tuple(int(b) for b in np.unique(bkt[inside]))

    def qmap(p, n):
        return (n, p * nb)

    def kmap_c(p, n):
        return (n, p * nb + 1)

    def kmap_p(p, n):
        return (jnp.maximum(n - 1, 0), p * nb + 1)

    def vmap_c(p, n):
        return (n, p * nb + 2)

    def vmap_p(p, n):
        return (jnp.maximum(n - 1, 0), p * nb + 2)

    blk = (bb, gw)
    o, lse = pl.pallas_call(
        functools.partial(_band_kernel, hpg=hpg, hd=hd, head0=g * hpg, buckets=buckets,
                          scale=float(hd) ** -0.5),
        grid=(dil, nblk),
        in_specs=[pl.BlockSpec(memory_space=pltpu.SMEM),
                  pl.BlockSpec((bb, 2 * bb), lambda p, n: (0, 0)),
                  pl.BlockSpec(blk, qmap), pl.BlockSpec(blk, kmap_p), pl.BlockSpec(blk, kmap_c),
                  pl.BlockSpec(blk, vmap_p), pl.BlockSpec(blk, vmap_c),
                  pl.BlockSpec((None, 1, hd), lambda p, n: (l, 0, 0)),
                  pl.BlockSpec((None, 1, hd), lambda p, n: (l, 0, 0))],
        out_specs=[pl.BlockSpec(blk, lambda p, n: (n, p)), pl.BlockSpec(blk, lambda p, n: (n, p))],
        out_shape=[jax.ShapeDtypeStruct((ls, dil * gw), BF16),
                   jax.ShapeDtypeStruct((ls, dil * gw), F32)],
        scratch_shapes=[pltpu.VMEM((hpg, bb, 2 * bb), F32)],
        compiler_params=_cparams(2, 32),
        name=f"band_attn_g{g}",
    )(rel_bias, jnp.asarray(bkt, I32), x, x, x, x, x, qn3, kn3)
    return o.reshape(S, gw), lse.reshape(S, gw)


def _alpha_kernel(*refs, ng, gw):
    o_refs, l_refs, y_ref = refs[:ng], refs[ng:2 * ng], refs[2 * ng]
    ls = [r[...] for r in l_refs]
    m = functools.reduce(jnp.maximum, ls)
    es = [jnp.exp(v - m) for v in ls]
    den = functools.reduce(lambda a, b: a + b, es)
    for g in range(ng):
        y_ref[:, g * gw:(g + 1) * gw] = (es[g] / den * o_refs[g][...].astype(F32)).astype(y_ref.dtype)


def alpha_merge(outs, lses, tm=512):
    ng = len(outs)
    S, gw = outs[0].shape
    tm = min(tm, S)
    spec = pl.BlockSpec((tm, gw), lambda i: (i, 0))
    return pl.pallas_call(
        functools.partial(_alpha_kernel, ng=ng, gw=gw),
        grid=(S // tm,),
        in_specs=[spec] * (2 * ng),
        out_specs=pl.BlockSpec((tm, ng * gw), lambda i: (i, 0)),
        out_shape=jax.ShapeDtypeStruct((S, ng * gw), BF16),
        compiler_params=_cparams(1, 32),
        name="alpha_merge",
    )(*outs, *lses)


def _cross_kernel(q_ref, k_ref, v_ref, qn_ref, kn_ref, o_ref, *, scale):
    q = _rms(q_ref[...].astype(F32), qn_ref[...]) * scale
    k = _rms(k_ref[...].astype(F32), kn_ref[...])
    s = lax.dot_general(q.astype(BF16), k.astype(BF16), (((1,), (1,)), ((), ())),
                        preferred_element_type=F32)
    m = jnp.max(s, axis=-1, keepdims=True)
    e = jnp.exp(s - m)
    pr = e / jnp.sum(e, axis=-1, keepdims=True)
    o_ref[...] = jnp.dot(pr.astype(BF16), v_ref[...], preferred_element_type=F32).astype(o_ref.dtype)


def cross_attention(proj_t, q_col0, kv, qn3, kn3, l, cfg, tm=1024):
    S = proj_t.shape[0]
    mlen = kv.shape[0]
    hm, hd = cfg.h_m, cfg.hd_m
    tm = min(tm, S)
    assert q_col0 % hd == 0
    qoff = q_col0 // hd
    return pl.pallas_call(
        functools.partial(_cross_kernel, scale=float(hd) ** -0.5),
        grid=(S // tm, hm),
        in_specs=[pl.BlockSpec((tm, hd), lambda i, h: (i, qoff + h)),
                  pl.BlockSpec((mlen, hd), lambda i, h: (0, h)),
                  pl.BlockSpec((mlen, hd), lambda i, h: (0, hm + h)),
                  pl.BlockSpec((None, 1, hd), lambda i, h: (l, 0, 0)),
                  pl.BlockSpec((None, 1, hd), lambda i, h: (l, 0, 0))],
        out_specs=pl.BlockSpec((tm, hd), lambda i, h: (i, h)),
        out_shape=jax.ShapeDtypeStruct((S, hm * hd), BF16),
        compiler_params=_cparams(2, 32),
        name="cross_attn",
    )(proj_t, kv, kv, qn3, kn3)


def _shift_rows(x, tail, s):
    rolled = pltpu.roll(x, s, axis=0)
    row = lax.broadcasted_iota(I32, (SUBLANES, x.shape[1]), 0)
    head = jnp.where(row >= s, rolled[:SUBLANES], pltpu.roll(tail, s, axis=0))
    return jnp.concatenate([head, rolled[SUBLANES:]], axis=0)


def _mlstm_kernel(q_ref, k_ref, v_ref, g_ref, ob_ref, cw_ref, cb_ref, gb_ref, hn_ref,
                  y_ref, cn_ref, ms_ref, tail_ref, *, nh, hd, dp, conv_w):
    c = pl.program_id(0)
    L = q_ref.shape[0]
    wb = nh * hd

    @pl.when(c == 0)
    def _():
        cn_ref[...] = jnp.zeros_like(cn_ref)
        ms_ref[...] = jnp.zeros_like(ms_ref)
        tail_ref[...] = jnp.zeros_like(tail_ref)

    def conv_silu(x_ref, col0):
        x = x_ref[...].astype(F32)
        tail = tail_ref[:, col0:col0 + wb]
        y = cb_ref[:, col0:col0 + wb] + cw_ref[conv_w - 1:conv_w, col0:col0 + wb] * x
        for s in range(1, conv_w):
            y = y + cw_ref[conv_w - 1 - s:conv_w - s, col0:col0 + wb] * _shift_rows(x, tail, s)
        tail_ref[:, col0:col0 + wb] = x[L - SUBLANES:]
        return y * _sigmoid(y)

    qs = conv_silu(q_ref, 0)
    ks = conv_silu(k_ref, wb) * (float(hd) ** -0.5)

    G = g_ref[...] + gb_ref[...]
    lf = jnp.minimum(G, 0.0) - jnp.log(1.0 + jnp.exp(-jnp.abs(G)))
    row = lax.broadcasted_iota(I32, (L, LANES), 0)
    F = lf
    sh = 1
    while sh < L:
        F = F + jnp.where(row >= sh, pltpu.roll(F, sh, axis=0), 0.0)
        sh *= 2
    GT = G.T
    FT = F.T
    ti = lax.broadcasted_iota(I32, (L, L), 0)
    si = lax.broadcasted_iota(I32, (L, L), 1)
    causal = ti >= si
    ones_col = (lax.broadcasted_iota(I32, (L, dp - hd), 1) == 0).astype(BF16)
    ms = ms_ref[...]

    for h in range(nh):
        sl = slice(h * hd, (h + 1) * hd)
        li_c, F_c = G[:, h:h + 1], F[:, nh + h:nh + h + 1]
        li_r, F_r = GT[h:h + 1, :], FT[nh + h:nh + h + 1, :]
        F_last = F_c[L - 1:L, :]
        m_prev = ms[:, h:h + 1]
        a_r = F_last - F_r + li_r
        b = jnp.max(a_r, axis=-1, keepdims=True)
        ea_c = jnp.exp(F_last - F_c + li_c - b)
        logw = jnp.where(causal, F_c - F_r + li_r, NEG)
        m_intra = jnp.max(logw, axis=-1, keepdims=True)
        m_inter = F_c + m_prev
        m_t = jnp.maximum(m_inter, m_intra)
        q = qs[:, sl].astype(BF16)
        k = ks[:, sl]
        v_aug = jnp.concatenate([v_ref[:, sl], ones_col], axis=1)
        s = lax.dot_general(q, k.astype(BF16), (((1,), (1,)), ((), ())),
                            preferred_element_type=F32) * jnp.exp(logw - m_t)
        inter = jnp.exp(m_inter - m_t)
        cn = cn_ref[h]
        num = (jnp.dot(s.astype(BF16), v_aug, preferred_element_type=F32)
               + inter * jnp.dot(q, cn.astype(BF16), preferred_element_type=F32))
        den = num[:, hd:hd + 1]
        hv = num[:, :hd] / jnp.maximum(jnp.abs(den), jnp.exp(-m_t))
        m_new = jnp.maximum(F_last + m_prev, b)
        decay = jnp.exp(F_last + m_prev - m_new)
        inj = jnp.exp(b - m_new)
        kv = jnp.dot((ea_c * k).T.astype(BF16), v_aug, preferred_element_type=F32)
        cn_ref[h] = decay * cn + inj * kv
        ms_ref[:, h:h + 1] = m_new
        hb = _rms(hv, hn_ref[:, sl])
        y_ref[:, sl] = (_sigmoid(ob_ref[:, sl].astype(F32)) * hb).astype(y_ref.dtype)


def mlstm(proj_b, gates, proj_t, conv_w, conv_b3, gbias3, hnorm3, l, cfg):
    S = proj_b.shape[0]
    L, nh, hd = cfg.chunk, cfg.h_b, cfg.hd_b
    wb = nh * hd
    dp = _round_up(hd + 1, LANES)
    assert L == LANES and S % L == 0 and 2 * nh <= LANES and cfg.conv_w <= SUBLANES
    cw = conv_w.shape[1]
    return pl.pallas_call(
        functools.partial(_mlstm_kernel, nh=nh, hd=hd, dp=dp, conv_w=cw),
        grid=(S // L,),
        in_specs=[pl.BlockSpec((L, wb), lambda c: (c, 0)),
                  pl.BlockSpec((L, wb), lambda c: (c, 1)),
                  pl.BlockSpec((L, wb), lambda c: (c, 2)),
                  pl.BlockSpec((L, LANES), lambda c: (c, 0)),
                  pl.BlockSpec((L, wb), lambda c: (c, 0)),
                  pl.BlockSpec((None, cw, 2 * wb), lambda c: (l, 0, 0)),
                  pl.BlockSpec((None, 1, 2 * wb), lambda c: (l, 0, 0)),
                  pl.BlockSpec((None, 1, LANES), lambda c: (l, 0, 0)),
                  pl.BlockSpec((None, 1, wb), lambda c: (l, 0, 0))],
        out_specs=pl.BlockSpec((L, wb), lambda c: (c, 0)),
        out_shape=jax.ShapeDtypeStruct((S, wb), BF16),
        scratch_shapes=[pltpu.VMEM((nh, hd, dp), F32), pltpu.VMEM((1, LANES), F32),
                        pltpu.VMEM((SUBLANES, 2 * wb), F32)],
        compiler_params=_cparams(1, 40),
        name="mlstm",
    )(proj_b, proj_b, proj_b, gates, proj_t, conv_w, conv_b3, gbias3, hnorm3)


def _merge_kernel(ya_ref, yb_ref, ym_ref, wa_ref, wb_ref, wm_ref, ga_ref, gb_ref, gm_ref,
                  o_ref, was, wbs, wms):
    @pl.when(pl.program_id(1) == 0)
    def _():
        was[...] = wa_ref[...].astype(BF16)
        wbs[...] = wb_ref[...].astype(BF16)
        wms[...] = wm_ref[...].astype(BF16)

    def branch(y_ref, w_ref, g_ref):
        return _sigmoid(g_ref[...].astype(F32)) * jnp.dot(y_ref[...], w_ref[...],
                                                          preferred_element_type=F32)

    o_ref[...] = (branch(ya_ref, was, ga_ref) + branch(yb_ref, wbs, gb_ref)
                  + branch(ym_ref, wms, gm_ref)).astype(o_ref.dtype)


def gated_merge(ya, yb, ym, w_a, w_b, w_m, proj_t, gate_col0, l, d, tm=1024, tn=512):
    S = ya.shape[0]
    tm = min(tm, S)
    while gate_col0 % tn or d % tn:
        tn //= 2
    goff = gate_col0 // tn
    nd = d // tn

    def lhs_spec(y):
        return pl.BlockSpec((tm, y.shape[1]), lambda n, m: (m, 0))

    def w_spec(w):
        return pl.BlockSpec((None, w.shape[1], tn), lambda n, m: (l, 0, n))

    def g_spec(j):
        return pl.BlockSpec((tm, tn), lambda n, m: (m, goff + j * nd + n))

    ksum = ya.shape[1] + yb.shape[1] + ym.shape[1]
    vmem = 2 * tm * ksum * 2 + 2 * ksum * tn * 4 + ksum * tn * 2 + 8 * tm * tn * 2 + 4 * tm * tn * 4
    return pl.pallas_call(
        _merge_kernel,
        grid=(nd, S // tm),
        in_specs=[lhs_spec(ya), lhs_spec(yb), lhs_spec(ym), w_spec(w_a), w_spec(w_b), w_spec(w_m),
                  g_spec(0), g_spec(1), g_spec(2)],
        out_specs=pl.BlockSpec((tm, tn), lambda n, m: (m, n)),
        out_shape=jax.ShapeDtypeStruct((S, d), BF16),
        scratch_shapes=[pltpu.VMEM((w.shape[1], tn), BF16) for w in (w_a, w_b, w_m)],
        compiler_params=_cparams(2, vmem // MIB + 6),
        name="gated_merge",
    )(ya, yb, ym, w_a, w_b, w_m, proj_t, proj_t, proj_t)


def _router_kernel(x_ref, g_ref, wr_ref, br_ref, h_ref, meta_ref, gate_ref, cnt_ref, run_ref, *, n_exp):
    i = pl.program_id(0)
    tm = x_ref.shape[0]

    @pl.when(i == 0)
    def _():
        run_ref[...] = jnp.zeros_like(run_ref)

    y = _rms(x_ref[...], g_ref[...])
    h_ref[...] = y
    logits = jnp.dot(y, wr_ref[...], preferred_element_type=F32,
                     precision=lax.Precision.HIGHEST) + br_ref[...]
    lane = lax.broadcasted_iota(I32, (tm, LANES), 1)
    lanef = lane.astype(F32)
    logits = jnp.where(lane < n_exp, logits, -jnp.inf)
    v1 = jnp.max(logits, axis=-1, keepdims=True)
    i1 = jnp.min(jnp.where(logits == v1, lanef, float(LANES)), axis=-1, keepdims=True).astype(I32)
    rest = jnp.where(lane == i1, -jnp.inf, logits)
    v2 = jnp.max(rest, axis=-1, keepdims=True)
    i2 = jnp.min(jnp.where(rest == v2, lanef, float(LANES)), axis=-1, keepdims=True).astype(I32)
    e = jnp.exp(v2 - v1)
    g1 = 1.0 / (1.0 + e)
    g2 = e / (1.0 + e)
    oh1 = lane == i1
    oh2 = lane == i2
    oh = jnp.where(oh1 | oh2, 1.0, 0.0)
    r = lax.broadcasted_iota(I32, (tm, tm), 0)
    cidx = lax.broadcasted_iota(I32, (tm, tm), 1)
    tri = jnp.where(cidx < r, 1.0, 0.0).astype(BF16)
    cum = jnp.dot(tri, oh.astype(BF16), preferred_element_type=F32) + run_ref[...]
    r1 = jnp.sum(jnp.where(oh1, cum, 0.0), axis=-1, keepdims=True).astype(I32)
    r2 = jnp.sum(jnp.where(oh2, cum, 0.0), axis=-1, keepdims=True).astype(I32)
    run_ref[...] += jnp.sum(oh, axis=0, keepdims=True)
    meta_ref[...] = jnp.where(lane == 0, i1, jnp.where(lane == 1, i2,
                              jnp.where(lane == 2, r1, jnp.where(lane == 3, r2, 0))))
    gate_ref[...] = jnp.where(lane == 0, g1, jnp.where(lane == 1, g2, 0.0))
    cnt_ref[...] = run_ref[...]


def norm_router(x, g3, w_router_p, b_router_p, l, lr, n_exp, tm=256):
    S, D = x.shape
    tm = min(tm, S)
    return pl.pallas_call(
        functools.partial(_router_kernel, n_exp=n_exp),
        grid=(S // tm,),
        in_specs=[pl.BlockSpec((tm, D), lambda i: (i, 0)),
                  pl.BlockSpec((None, 1, D), lambda i: (l, 0, 0)),
                  pl.BlockSpec((None, D, LANES), lambda i: (lr, 0, 0)),
                  pl.BlockSpec((None, 1, LANES), lambda i: (lr, 0, 0))],
        out_specs=[pl.BlockSpec((tm, D), lambda i: (i, 0)),
                   pl.BlockSpec((tm, LANES), lambda i: (i, 0)),
                   pl.BlockSpec((tm, LANES), lambda i: (i, 0)),
                   pl.BlockSpec((1, LANES), lambda i: (0, 0))],
        out_shape=[jax.ShapeDtypeStruct((S, D), F32), jax.ShapeDtypeStruct((S, LANES), I32),
                   jax.ShapeDtypeStruct((S, LANES), F32), jax.ShapeDtypeStruct((1, LANES), F32)],
        scratch_shapes=[pltpu.VMEM((1, LANES), F32)],
        compiler_params=_cparams(1, 40),
        name="norm_router",
    )(x, g3, w_router_p, b_router_p)


def _row_copy(src_hbm, dst, src_row, dst_row, sem):
    return pltpu.make_async_copy(src_hbm.at[pl.ds(src_row, 1)], dst.at[pl.ds(dst_row, 1)], sem)


def _gather_kernel(order_ref, nv_ref, h_hbm, xs_ref, buf, sem, *, tg):
    i = pl.program_id(0)
    nv = nv_ref[0]
    slot = i % 2

    def start_tile(t, s):
        def body(r, carry):
            _row_copy(h_hbm, buf.at[s], order_ref[t * tg + r], r, sem.at[s]).start()
            return carry
        lax.fori_loop(0, tg, body, 0, unroll=8)

    def wait_tile(s):
        def body(r, carry):
            _row_copy(h_hbm, buf.at[s], 0, r, sem.at[s]).wait()
            return carry
        lax.fori_loop(0, tg, body, 0, unroll=8)

    @pl.when(i == 0)
    def _():
        start_tile(0, 0)

    @pl.when(i + 1 < nv)
    def _():
        start_tile(i + 1, 1 - slot)

    @pl.when(i < nv)
    def _():
        wait_tile(slot)
        xs_ref[...] = buf[slot].astype(BF16)

    @pl.when(i >= nv)
    def _():
        xs_ref[...] = jnp.zeros_like(xs_ref)


def gather_rows(h, order, nvalid, n_tiles, tg):
    D = h.shape[1]
    grid_spec = pltpu.PrefetchScalarGridSpec(
        num_scalar_prefetch=2,
        grid=(n_tiles,),
        in_specs=[pl.BlockSpec(memory_space=pl.ANY)],
        out_specs=pl.BlockSpec((tg, D), lambda i, order, nv: (i, 0)),
        scratch_shapes=[pltpu.VMEM((2, tg, D), F32), pltpu.SemaphoreType.DMA((2,))],
    )
    return pl.pallas_call(
        functools.partial(_gather_kernel, tg=tg),
        grid_spec=grid_spec,
        out_shape=jax.ShapeDtypeStruct((n_tiles * tg, D), BF16),
        compiler_params=_cparams(1, 40),
        name="moe_gather",
    )(order, nvalid, h)


def _gmm_kernel(te_ref, src_ref, first_ref, nxt_ref, lastrun_ref, nv_ref, *refs, nw, epi, lr, tn):
    x_ref = refs[0]
    w_hbm = refs[1:1 + nw]
    out_ref = refs[1 + nw]
    stage, wbf, sem = refs[2 + nw:5 + nw]
    n = pl.program_id(0)
    i = pl.program_id(1)

    def w_copy(j, e, nb):
        return pltpu.make_async_copy(w_hbm[j].at[lr, e, :, pl.ds(nb * tn, tn)], stage.at[j], sem.at[j])

    @pl.when((n == 0) & (i == 0))
    def _():
        for j in range(nw):
            w_copy(j, te_ref[0], 0).start()

    @pl.when(first_ref[i] == 1)
    def _():
        for j in range(nw):
            w_copy(j, te_ref[i], n).wait()
            wbf[j] = stage[j].astype(BF16)
        nb_next = n + lastrun_ref[i]

        @pl.when(nb_next < pl.num_programs(0))
        def _():
            for j in range(nw):
                w_copy(j, nxt_ref[i], nb_next).start()

    @pl.when(i < nv_ref[0])
    def _():
        x = x_ref[...]
        accs = [jnp.dot(x, wbf[j], preferred_element_type=F32) for j in range(nw)]
        if epi == "swiglu":
            g, u = accs
            out = g * _sigmoid(g) * u
        else:
            out = accs[0]
        out_ref[...] = out.astype(out_ref.dtype)

    @pl.when(i >= nv_ref[0])
    def _():
        out_ref[...] = jnp.zeros_like(out_ref)


def grouped_matmul(xs, ws, lr, tile_meta, *, epi, out_dtype, tg, tn, name):
    P, K = xs.shape
    N = ws[0].shape[-1]
    nw = len(ws)
    tn = min(tn, N)
    assert N % tn == 0 and P % tg == 0
    grid_spec = pltpu.PrefetchScalarGridSpec(
        num_scalar_prefetch=len(tile_meta),
        grid=(N // tn, P // tg),
        in_specs=[pl.BlockSpec((tg, K), lambda n, i, te, src, *_: (src[i], 0))]
        + [pl.BlockSpec(memory_space=pl.ANY)] * nw,
        out_specs=pl.BlockSpec((tg, tn), lambda n, i, *_: (i, n)),
        scratch_shapes=[pltpu.VMEM((nw, K, tn), F32), pltpu.VMEM((nw, K, tn), BF16),
                        pltpu.SemaphoreType.DMA((nw,))],
    )
    obytes = jnp.dtype(out_dtype).itemsize
    vmem = 2 * tg * K * 2 + nw * K * tn * 6 + 2 * tg * tn * obytes + (nw + 1) * tg * tn * 4
    return pl.pallas_call(
        functools.partial(_gmm_kernel, nw=nw, epi=epi, lr=lr, tn=tn),
        grid_spec=grid_spec,
        out_shape=jax.ShapeDtypeStruct((P, N), out_dtype),
        compiler_params=_cparams(2, vmem // MIB + 6),
        name=name,
    )(*tile_meta, xs, *ws)


def _combine_kernel(dest_ref, x_ref, gate_ref, ys_hbm, out_ref, buf, sem, *, tc, top_k):
    i = pl.program_id(0)
    slot = i % 2

    def start_tile(t, s):
        def body(r, carry):
            for k in range(top_k):
                _row_copy(ys_hbm, buf.at[s, k], dest_ref[(t * tc + r) * top_k + k], r, sem.at[s]).start()
            return carry
        lax.fori_loop(0, tc, body, 0, unroll=4)

    def wait_tile(s):
        def body(r, carry):
            for k in range(top_k):
                _row_copy(ys_hbm, buf.at[s, k], 0, r, sem.at[s]).wait()
            return carry
        lax.fori_loop(0, tc, body, 0, unroll=4)

    @pl.when(i == 0)
    def _():
        start_tile(0, 0)

    @pl.when(i + 1 < pl.num_programs(0))
    def _():
        start_tile(i + 1, 1 - slot)

    wait_tile(slot)
    g = gate_ref[...]
    out = x_ref[...]
    for k in range(top_k):
        out = out + g[:, k:k + 1] * buf[slot, k]
    out_ref[...] = out


def moe_combine(x, gates, ys, dest_flat, top_k, tc=256):
    S, D = x.shape
    tc = min(tc, S)
    grid_spec = pltpu.PrefetchScalarGridSpec(
        num_scalar_prefetch=1,
        grid=(S // tc,),
        in_specs=[pl.BlockSpec((tc, D), lambda i, d: (i, 0)),
                  pl.BlockSpec((tc, LANES), lambda i, d: (i, 0)),
                  pl.BlockSpec(memory_space=pl.ANY)],
        out_specs=pl.BlockSpec((tc, D), lambda i, d: (i, 0)),
        scratch_shapes=[pltpu.VMEM((2, top_k, tc, D), F32), pltpu.SemaphoreType.DMA((2,))],
    )
    return pl.pallas_call(
        functools.partial(_combine_kernel, tc=tc, top_k=top_k),
        grid_spec=grid_spec,
        out_shape=jax.ShapeDtypeStruct((S, D), F32),
        compiler_params=_cparams(1, 48),
        name="moe_combine",
    )(dest_flat, x, gates, ys)


def moe_layer(x, norm_ffn3, l, w_router, b_router, w_e_gate, w_e_up, w_e_down, lr, cfg):
    S, D = x.shape
    E, top_k, tg = cfg.n_experts, 2, cfg.tg
    tg = min(tg, S)
    wr = jnp.pad(w_router, ((0, 0), (0, 0), (0, LANES - E)))
    br = jnp.pad(b_router, ((0, 0), (0, LANES - E)))[:, None, :]
    h, meta, gates, cnt = norm_router(x, norm_ffn3, wr, br, l, lr, E)

    eid, rank = meta[:, 0:top_k], meta[:, top_k:2 * top_k]
    counts = cnt[0, :E].astype(I32)
    padded = (counts + tg - 1) // tg * tg
    ends = jnp.cumsum(padded)
    dest = (ends - padded)[eid] + rank
    n_tiles = (S * top_k) // tg + E
    order = jnp.zeros((n_tiles * tg,), I32).at[dest.reshape(-1)].set(
        jnp.repeat(jnp.arange(S, dtype=I32), top_k))
    nvalid = (ends[-1] // tg).astype(I32)
    tile = jnp.arange(n_tiles, dtype=I32)
    src = jnp.minimum(tile, nvalid - 1)
    te = jnp.sum((src * tg)[:, None] >= ends[None, :], axis=1).astype(I32)
    first = ((tile == 0) | (te != jnp.roll(te, 1))).astype(I32)
    later = (tile[None, :] > tile[:, None]) & (te[None, :] != te[:, None])
    nxt_idx = jnp.min(jnp.where(later, tile[None, :], n_tiles), axis=1)
    lastrun = (nxt_idx == n_tiles).astype(I32)
    nxt = jnp.where(lastrun == 1, te[0], te[jnp.minimum(nxt_idx, n_tiles - 1)]).astype(I32)
    nv = nvalid.reshape(1)
    tile_meta = (te, src, first, nxt, lastrun, nv)

    xs = gather_rows(h, order, nv, n_tiles, tg)
    a = grouped_matmul(xs, [w_e_gate, w_e_up], lr, tile_meta, epi="swiglu", out_dtype=BF16,
                       tg=tg, tn=cfg.tn_b, name="moe_gate_up")
    ys = grouped_matmul(a, [w_e_down], lr, tile_meta, epi="plain", out_dtype=F32,
                        tg=tg, tn=cfg.tn_gd, name="moe_down")
    return moe_combine(x, gates, ys, dest.reshape(-1), top_k)


def _forward(x, mem, rel_bias, norm_mix, norm_ffn, norm_mem, w_in, qn_a, kn_a, conv_w, conv_b,
             gate_bias_b, hnorm_b, w_mem_kv, qn_m, kn_m, w_br_a, w_br_b, w_br_m, w_out,
             w_ff_gate, w_ff_up, w_ff_down, w_router, b_router, w_e_gate, w_e_up, w_e_down, cfg):
    B, S, D = x.shape
    assert B == 1 and mem.shape[0] == 1
    depth = norm_mix.shape[0]
    ng = len(cfg.a_groups)
    w_a = ng * cfg.heads_per_group * cfg.hd_a
    w_b = cfg.h_b * cfg.hd_b
    w_m = cfg.h_m * cfg.hd_m
    if_col0 = 3 * w_a + 3 * w_b
    tail_col0 = if_col0 + 2 * cfg.h_b
    n_tail = w_b + w_m + 3 * D
    d_ff = w_ff_gate.shape[-1]
    d_ff_p = _round_up(d_ff, cfg.tk_down) if d_ff > cfg.tk_down else _round_up(d_ff, LANES)

    def row3(p):
        return p[:, None, :]

    x = x.reshape(S, D)
    mem2 = mem.reshape(mem.shape[1], D)
    norm_mix3, norm_ffn3, norm_mem3 = row3(norm_mix), row3(norm_ffn), row3(norm_mem)
    qn_a3, kn_a3, qn_m3, kn_m3 = row3(qn_a), row3(kn_a), row3(qn_m), row3(kn_m)
    conv_b3, hnorm3 = row3(conv_b), row3(hnorm_b)
    gbias3 = row3(jnp.pad(gate_bias_b, ((0, 0), (0, LANES - 2 * cfg.h_b))))
    mm = functools.partial(matmul, tm=cfg.tm)
    gw = cfg.heads_per_group * cfg.hd_a
    w_in_t = jnp.swapaxes(w_in, 1, 2)

    for l in range(depth):
        h = rmsnorm(x, norm_mix3, l)
        proj_g = [matmul(h, [w_in_t], lead=l, trans=True, n_out=3 * gw, tm=cfg.tm_b, tn=gw,
                         blk_of=lambda n, g=g: n * ng + g, name=f"proj_a{g}")
                  for g in range(ng)]
        proj_b = matmul(h, [w_in_t], lead=l, trans=True, col0=3 * w_a, n_out=3 * w_b,
                        tm=cfg.tm_b, tn=cfg.tn_b, name="proj_b")
        gates_b = mm(h, [w_in_t], lead=l, trans=True, col0=if_col0, n_out=LANES, out_dtype=F32,
                     tn=LANES, name="proj_if")
        proj_t = matmul(h, [w_in_t], lead=l, trans=True, col0=tail_col0, n_out=w_b + w_m,
                        tm=cfg.tm_b, tn=cfg.tn_b, name="proj_tail")
        proj_gt = mm(h, [w_in_t], lead=l, trans=True, col0=tail_col0 + w_b + w_m, n_out=3 * D,
                     tn=cfg.tn, name="proj_gates")

        outs, lses = zip(*[band_attention(proj_g[g], rel_bias, qn_a3, kn_a3, l, g, cfg)
                           for g in range(ng)])
        y_a = alpha_merge(outs, lses)
        y_b = mlstm(proj_b, gates_b, proj_t, conv_w, conv_b3, gbias3, hnorm3, l, cfg)
        hm = rmsnorm(mem2, norm_mem3, l)
        kv = mm(hm, [w_mem_kv], lead=l, n_out=2 * w_m, tn=cfg.tn, name="mem_kv")
        y_m = cross_attention(proj_t, w_b, kv, qn_m3, kn_m3, l, cfg)
        y = gated_merge(y_a, y_b, y_m, w_br_a, w_br_b, w_br_m, proj_gt, 0, l, D,
                        tm=cfg.tm_b // 2, tn=cfg.tn_b)
        x = matmul(y, [w_out], lead=l, n_out=D, epi="residual", res=x, out_dtype=F32,
                   tm=cfg.tm_b, tn=cfg.tn_b, name="out_proj")

        if l % 2 == 0:
            ld = l // 2
            h2 = rmsnorm(x, norm_ffn3, l)
            a = mm(h2, [w_ff_gate, w_ff_up], lead=ld, n_out=d_ff_p, epi="swiglu", n_valid=d_ff,
                   tn=cfg.tn2, name="ffn_gate_up")
            wd = jnp.pad(w_ff_down[ld], ((0, d_ff_p - d_ff), (0, 0))).astype(BF16)
            x = matmul_ktiled_residual(a, wd, x, tm=cfg.tm_b, tn=1024, tk=cfg.tk_down, name="ffn_down")
        else:
            x = moe_layer(x, norm_ffn3, l, w_router, b_router, w_e_gate, w_e_up, w_e_down, l // 2, cfg)
    return x.reshape(B, S, D)


def kernel(x, mem, rel_bias, norm_mix, norm_ffn, norm_mem, w_in, qn_a, kn_a, conv_w, conv_b, gate_bias_b, hnorm_b, w_mem_kv, qn_m, kn_m, w_br_a, w_br_b, w_br_m, w_out, w_ff_gate, w_ff_up, w_ff_down, w_router, b_router, w_e_gate, w_e_up, w_e_down):
    return _forward(x, mem, rel_bias, norm_mix, norm_ffn, norm_mem, w_in, qn_a, kn_a, conv_w, conv_b,
                    gate_bias_b, hnorm_b, w_mem_kv, qn_m, kn_m, w_br_a, w_br_b, w_br_m, w_out,
                    w_ff_gate, w_ff_up, w_ff_down, w_router, b_router, w_e_gate, w_e_up, w_e_down,
                    Cfg())
```

```python
import functools
from typing import NamedTuple

import numpy as np
import jax
import jax.numpy as jnp
from jax import lax
from jax.experimental import pallas as pl
from jax.experimental.pallas import tpu as pltpu

F32 = jnp.float32
BF16 = jnp.bfloat16
I32 = jnp.int32
EPS = 1e-6
NEG = -1e30
MIB = 1 << 20
LANES = 128
SUBLANES = 8
VMEM_CAP_MIB = 60
VMEM_SLACK_MIB = 12


class Cfg(NamedTuple):
    a_groups: tuple = ((128, 1), (512, 4), (2048, 16))
    heads_per_group: int = 4
    hd_a: int = 128
    band_block: int = 128
    h_b: int = 4
    hd_b: int = 384
    chunk: int = 128
    conv_w: int = 4
    h_m: int = 4
    hd_m: int = 256
    n_buckets: int = 32
    max_dist: int = 2048
    n_experts: int = 8
    tm: int = 1024
    tn: int = 1024
    tn_b: int = 512
    tn2: int = 512
    tn_gd: int = 1024
    tg: int = 512
    tk_down: int = 5504
    tn_down: int = 512


def _cparams(n_axes, vmem_mib):
    return pltpu.CompilerParams(dimension_semantics=("arbitrary",) * n_axes,
                                vmem_limit_bytes=int(min(vmem_mib, VMEM_CAP_MIB)) * MIB)


def _round_up(a, b):
    return -(-a // b) * b


def _sigmoid(x):
    return 1.0 / (1.0 + jnp.exp(-x))


def _rms(x, g):
    return x * lax.rsqrt(jnp.mean(x * x, axis=-1, keepdims=True) + EPS) * g


def _rmsnorm_kernel(x_ref, g_ref, o_ref):
    o_ref[...] = _rms(x_ref[...].astype(F32), g_ref[...]).astype(o_ref.dtype)


def rmsnorm(x, g3, l, out_dtype=BF16, tm=256):
    M, D = x.shape
    tm = min(tm, M)
    return pl.pallas_call(
        _rmsnorm_kernel,
        grid=(M // tm,),
        in_specs=[pl.BlockSpec((tm, D), lambda i: (i, 0)),
                  pl.BlockSpec((None, 1, D), lambda i: (l, 0, 0))],
        out_specs=pl.BlockSpec((tm, D), lambda i: (i, 0)),
        out_shape=jax.ShapeDtypeStruct((M, D), out_dtype),
        compiler_params=_cparams(1, 32),
        name="rmsnorm",
    )(x, g3)


def _cast_blocks(stage, wbf, nw, rows=512):
    n_rows = stage.shape[1]
    for j in range(nw):
        for r in range(0, n_rows, rows):
            sl = slice(r, min(r + rows, n_rows))
            wbf[j, sl] = stage[j, sl].astype(BF16)


def _mm_kernel(lhs_ref, *refs, nw, trans, epi, n_valid, tn, gn, last_w, src_of):
    w_hbm = refs[:nw]
    pos = nw
    res_ref = None
    if epi == "residual":
        res_ref = refs[pos]
        pos += 1
    out_ref = refs[pos]
    stage, wbf, sem = refs[pos + 1:pos + 4]
    n = pl.program_id(0)
    m = pl.program_id(1)

    def copies(nb, width):
        out = []
        for j in range(nw):
            if width == tn:
                dst = stage.at[j]
            elif trans:
                dst = stage.at[j, pl.ds(0, width), :]
            else:
                dst = stage.at[j, :, pl.ds(0, width)]
            out.append(pltpu.make_async_copy(src_of(w_hbm[j], nb, width), dst, sem.at[j]))
        return out

    def for_block(nb, fn):
        if last_w == tn:
            for c in copies(nb, tn):
                fn(c)
        elif isinstance(nb, int):
            for c in copies(nb, last_w if nb == gn - 1 else tn):
                fn(c)
        else:
            @pl.when(nb == gn - 1)
            def _():
                for c in copies(nb, last_w):
                    fn(c)

            @pl.when(nb != gn - 1)
            def _():
                for c in copies(nb, tn):
                    fn(c)

    @pl.when((n == 0) & (m == 0))
    def _():
        for_block(0, lambda c: c.start())

    @pl.when(m == 0)
    def _():
        for_block(n, lambda c: c.wait())
        _cast_blocks(stage, wbf, nw)

        @pl.when(n + 1 < gn)
        def _():
            for_block(n + 1, lambda c: c.start())

    lhs = lhs_ref[...]
    dims = (((1,), (1,)), ((), ())) if trans else (((1,), (0,)), ((), ()))
    accs = [lax.dot_general(lhs, wbf[j], dims, preferred_element_type=F32) for j in range(nw)]
    if epi == "plain":
        out = accs[0]
    elif epi == "residual":
        out = res_ref[...] + accs[0]
    else:
        g, u = accs
        out = g * _sigmoid(g) * u
        if n_valid is not None:
            col = pl.program_id(0) * tn + lax.broadcasted_iota(I32, out.shape, 1)
            out = jnp.where(col < n_valid, out, 0.0)
    out_ref[...] = out.astype(out_ref.dtype)


def matmul(lhs, ws, *, lead=None, col0=0, n_out, epi="plain", res=None, n_valid=None,
           trans=False, blk_of=None, out_dtype=BF16, tm=1024, tn=512, name="matmul"):
    M, K = lhs.shape
    nw = len(ws)
    tm = min(tm, M)
    tn = min(tn, _round_up(n_out, LANES))
    assert tn % LANES == 0 and M % tm == 0 and ws[0].dtype == F32
    gn = pl.cdiv(n_out, tn)
    last_w = n_out - (gn - 1) * tn
    if blk_of is None:
        def col_of(nb):
            return col0 + nb * tn
    else:
        def col_of(nb):
            return blk_of(nb) * tn
    if trans:
        n_rows = ws[0].shape[1]
        assert lead is not None and n_rows % SUBLANES == 0 and col0 % SUBLANES == 0 and last_w % SUBLANES == 0
        ws = [w.reshape(-1, K) for w in ws]
        w_block = (tn, K)

        def src_of(w, nb, width):
            return w.at[pl.ds(pl.multiple_of(lead * n_rows + col_of(nb), SUBLANES), width), :]
    else:
        n_cols = ws[0].shape[-1]
        assert col0 % LANES == 0 and last_w % LANES == 0 and n_cols % LANES == 0
        assert col0 + n_out <= n_cols or n_valid is not None
        w_block = (K, tn)

        def src_of(w, nb, width):
            w = w if lead is None else w.at[lead]
            col = jnp.minimum(col_of(nb), n_cols - width)
            return w.at[:, pl.ds(pl.multiple_of(col, LANES), width)]
    in_specs = [pl.BlockSpec((tm, K), lambda n, m: (m, 0))] + [pl.BlockSpec(memory_space=pl.ANY)] * nw
    args = [lhs] + list(ws)
    if epi == "residual":
        in_specs.append(pl.BlockSpec((tm, tn), lambda n, m: (m, n)))
        args.append(res)
    obytes = jnp.dtype(out_dtype).itemsize
    vmem = (3 * tm * K * 2 + nw * K * tn * 6 + 2 * tm * tn * obytes
            + (2 * tm * tn * 4 if epi == "residual" else 0) + (nw + 1) * tm * tn * 4)
    return pl.pallas_call(
        functools.partial(_mm_kernel, nw=nw, trans=trans, epi=epi, n_valid=n_valid, tn=tn, gn=gn,
                          last_w=last_w, src_of=src_of),
        grid=(gn, M // tm),
        in_specs=in_specs,
        out_specs=pl.BlockSpec((tm, tn), lambda n, m: (m, n)),
        out_shape=jax.ShapeDtypeStruct((M, n_out), out_dtype),
        scratch_shapes=[pltpu.VMEM((nw,) + w_block, F32), pltpu.VMEM((nw,) + w_block, BF16),
                        pltpu.SemaphoreType.DMA((nw,))],
        compiler_params=_cparams(2, vmem // MIB + VMEM_SLACK_MIB),
        name=name,
    )(*args)


def _mmk_kernel(lhs_ref, w_ref, res_ref, out_ref, acc_ref):
    k = pl.program_id(2)

    @pl.when(k == 0)
    def _():
        acc_ref[...] = jnp.zeros_like(acc_ref)

    acc_ref[...] += jnp.dot(lhs_ref[...], w_ref[...], preferred_element_type=F32)

    @pl.when(k == pl.num_programs(2) - 1)
    def _():
        out_ref[...] = res_ref[...] + acc_ref[...]


def matmul_ktiled_residual(lhs, w, res, *, tm=1024, tn=1024, tk=1024, name="matmul_k"):
    M, K = lhs.shape
    N = w.shape[1]
    tm, tn, tk = min(tm, M), min(tn, N), min(tk, K)
    assert M % tm == 0 and N % tn == 0 and K % tk == 0
    vmem = 2 * tm * tk * 2 + 2 * tk * tn * 2 + 5 * tm * tn * 4
    return pl.pallas_call(
        _mmk_kernel,
        grid=(M // tm, N // tn, K // tk),
        in_specs=[pl.BlockSpec((tm, tk), lambda m, n, k: (m, k)),
                  pl.BlockSpec((tk, tn), lambda m, n, k: (k, n)),
                  pl.BlockSpec((tm, tn), lambda m, n, k: (m, n))],
        out_specs=pl.BlockSpec((tm, tn), lambda m, n, k: (m, n)),
        out_shape=jax.ShapeDtypeStruct((M, N), F32),
        scratch_shapes=[pltpu.VMEM((tm, tn), F32)],
        compiler_params=_cparams(3, vmem // MIB + VMEM_SLACK_MIB),
        name=name,
    )(lhs, w, res)


def _t5_bucket_np(dist, n_buckets, max_dist):
    max_exact = n_buckets // 2
    d = np.maximum(dist, 1).astype(np.float32)
    large = max_exact + (np.log(d / np.float32(max_exact)) / np.float32(np.log(max_dist / max_exact))
                         * np.float32(n_buckets - max_exact)).astype(np.int32)
    large = np.minimum(large, n_buckets - 1)
    return np.where(dist < max_exact, dist, large).astype(np.int32)


def _band_kernel(tab_ref, bkt_ref, q_ref, kp_ref, kc_ref, vp_ref, vc_ref, qn_ref, kn_ref,
                 o_ref, lse_ref, bias_ref, *, hpg, hd, head0, buckets, scale):
    p = pl.program_id(0)
    n = pl.program_id(1)
    bb = q_ref.shape[0]

    @pl.when((p == 0) & (n == 0))
    def _():
        bkt = bkt_ref[...]
        for h in range(hpg):
            acc = jnp.full((bb, 2 * bb), NEG, F32)
            for b in buckets:
                acc = jnp.where(bkt == b, tab_ref[b, head0 + h], acc)
            bias_ref[h] = acc

    ki = lax.broadcasted_iota(I32, (bb, 2 * bb), 1)
    kvalid = (ki >= bb) | (n > 0)
    for h in range(hpg):
        sl = slice(h * hd, (h + 1) * hd)
        q = _rms(q_ref[:, sl].astype(F32), qn_ref[...]) * scale
        k = jnp.concatenate([kp_ref[:, sl], kc_ref[:, sl]], axis=0).astype(F32)
        k = _rms(k, kn_ref[...])
        v = jnp.concatenate([vp_ref[:, sl], vc_ref[:, sl]], axis=0)
        s = lax.dot_general(q.astype(BF16), k.astype(BF16), (((1,), (1,)), ((), ())),
                            preferred_element_type=F32)
        s = jnp.where(kvalid, s + bias_ref[h], NEG)
        m = jnp.max(s, axis=-1, keepdims=True)
        e = jnp.exp(s - m)
        l = jnp.sum(e, axis=-1, keepdims=True)
        o = jnp.dot((e / l).astype(BF16), v, preferred_element_type=F32)
        o_ref[:, sl] = o.astype(o_ref.dtype)
        lse_ref[:, sl] = jnp.broadcast_to(m + jnp.log(l), (bb, hd))


def band_attention(proj_g, rel_bias, qn3, kn3, l, g, cfg):
    S, npa = proj_g.shape
    win, dil = cfg.a_groups[g]
    steps = win // dil
    bb, hpg, hd = cfg.band_block, cfg.heads_per_group, cfg.hd_a
    gw = hpg * hd
    ls = S // dil
    assert S % dil == 0 and ls % bb == 0 and steps <= bb and npa == 3 * gw
    nblk = ls // bb
    nb = 3
    x = proj_g.reshape(ls, dil * npa)

    qi = np.arange(bb)[:, None]
    ki = np.arange(2 * bb)[None, :]
    rel = qi + bb - ki
    inside = (rel >= 0) & (rel <= steps)
    bkt = np.where(inside, _t5_bucket_np(np.maximum(rel, 0) * dil, cfg.n_buckets, cfg.max_dist), -1)
    buckets = tuple(int(b) for b in np.unique(bkt[inside]))

    def qmap(p, n):
        return (n, p * nb)

    def kmap_c(p, n):
        return (n, p * nb + 1)

    def kmap_p(p, n):
        return (jnp.maximum(n - 1, 0), p * nb + 1)

    def vmap_c(p, n):
        return (n, p * nb + 2)

    def vmap_p(p, n):
        return (jnp.maximum(n - 1, 0), p * nb + 2)

    blk = (bb, gw)
    o, lse = pl.pallas_call(
        functools.partial(_band_kernel, hpg=hpg, hd=hd, head0=g * hpg, buckets=buckets,
                          scale=float(hd) ** -0.5),
        grid=(dil, nblk),
        in_specs=[pl.BlockSpec(memory_space=pltpu.SMEM),
                  pl.BlockSpec((bb, 2 * bb), lambda p, n: (0, 0)),
                  pl.BlockSpec(blk, qmap), pl.BlockSpec(blk, kmap_p), pl.BlockSpec(blk, kmap_c),
                  pl.BlockSpec(blk, vmap_p), pl.BlockSpec(blk, vmap_c),
                  pl.BlockSpec((None, 1, hd), lambda p, n: (l, 0, 0)),
                  pl.BlockSpec((None, 1, hd), lambda p, n: (l, 0, 0))],
        out_specs=[pl.BlockSpec(blk, lambda p, n: (n, p)), pl.BlockSpec(blk, lambda p, n: (n, p))],
        out_shape=[jax.ShapeDtypeStruct((ls, dil * gw), BF16),
                   jax.ShapeDtypeStruct((ls, dil * gw), F32)],
        scratch_shapes=[pltpu.VMEM((hpg, bb, 2 * bb), F32)],
        compiler_params=_cparams(2, 32),
        name=f"band_attn_g{g}",
    )(rel_bias, jnp.asarray(bkt, I32), x, x, x, x, x, qn3, kn3)
    return o.reshape(S, gw), lse.reshape(S, gw)


def _alpha_kernel(*refs, ng, gw):
    o_refs, l_refs, y_ref = refs[:ng], refs[ng:2 * ng], refs[2 * ng]
    ls = [r[...] for r in l_refs]
    m = functools.reduce(jnp.maximum, ls)
    es = [jnp.exp(v - m) for v in ls]
    den = functools.reduce(lambda a, b: a + b, es)
    for g in range(ng):
        y_ref[:, g * gw:(g + 1) * gw] = (es[g] / den * o_refs[g][...].astype(F32)).astype(y_ref.dtype)


def alpha_merge(outs, lses, tm=512):
    ng = len(outs)
    S, gw = outs[0].shape
    tm = min(tm, S)
    spec = pl.BlockSpec((tm, gw), lambda i: (i, 0))
    return pl.pallas_call(
        functools.partial(_alpha_kernel, ng=ng, gw=gw),
        grid=(S // tm,),
        in_specs=[spec] * (2 * ng),
        out_specs=pl.BlockSpec((tm, ng * gw), lambda i: (i, 0)),
        out_shape=jax.ShapeDtypeStruct((S, ng * gw), BF16),
        compiler_params=_cparams(1, 32),
        name="alpha_merge",
    )(*outs, *lses)


def _cross_kernel(q_ref, k_ref, v_ref, qn_ref, kn_ref, o_ref, *, scale):
    q = _rms(q_ref[...].astype(F32), qn_ref[...]) * scale
    k = _rms(k_ref[...].astype(F32), kn_ref[...])
    s = lax.dot_general(q.astype(BF16), k.astype(BF16), (((1,), (1,)), ((), ())),
                        preferred_element_type=F32)
    m = jnp.max(s, axis=-1, keepdims=True)
    e = jnp.exp(s - m)
    pr = e / jnp.sum(e, axis=-1, keepdims=True)
    o_ref[...] = jnp.dot(pr.astype(BF16), v_ref[...], preferred_element_type=F32).astype(o_ref.dtype)


def cross_attention(proj_t, q_col0, kv, qn3, kn3, l, cfg, tm=1024):
    S = proj_t.shape[0]
    mlen = kv.shape[0]
    hm, hd = cfg.h_m, cfg.hd_m
    tm = min(tm, S)
    assert q_col0 % hd == 0
    qoff = q_col0 // hd
    return pl.pallas_call(
        functools.partial(_cross_kernel, scale=float(hd) ** -0.5),
        grid=(S // tm, hm),
        in_specs=[pl.BlockSpec((tm, hd), lambda i, h: (i, qoff + h)),
                  pl.BlockSpec((mlen, hd), lambda i, h: (0, h)),
                  pl.BlockSpec((mlen, hd), lambda i, h: (0, hm + h)),
                  pl.BlockSpec((None, 1, hd), lambda i, h: (l, 0, 0)),
                  pl.BlockSpec((None, 1, hd), lambda i, h: (l, 0, 0))],
        out_specs=pl.BlockSpec((tm, hd), lambda i, h: (i, h)),
        out_shape=jax.ShapeDtypeStruct((S, hm * hd), BF16),
        compiler_params=_cparams(2, 32),
        name="cross_attn",
    )(proj_t, kv, kv, qn3, kn3)


def _mlstm_kernel(q_ref, k_ref, v_ref, g_ref, ob_ref, cw_ref, cb_ref, gb_ref, hn_ref,
                  y_ref, cn_ref, ms_ref, xw_ref, *, nh, hd, dp, conv_w):
    c = pl.program_id(0)
    L = q_ref.shape[0]
    wb = nh * hd

    @pl.when(c == 0)
    def _():
        cn_ref[...] = jnp.zeros_like(cn_ref)
        ms_ref[...] = jnp.zeros_like(ms_ref)
        xw_ref[:SUBLANES] = jnp.zeros((SUBLANES, 2 * wb), F32)

    def conv_silu(x_ref, col0):
        cols = slice(col0, col0 + wb)
        x = x_ref[...].astype(F32)
        xw_ref[SUBLANES:, cols] = x
        y = cb_ref[:, cols] + cw_ref[conv_w - 1:conv_w, cols] * x
        for s in range(1, conv_w):
            y = y + cw_ref[conv_w - 1 - s:conv_w - s, cols] * xw_ref[SUBLANES - s:SUBLANES - s + L, cols]
        xw_ref[:SUBLANES, cols] = x[L - SUBLANES:]
        return y * _sigmoid(y)

    qs = conv_silu(q_ref, 0)
    ks = conv_silu(k_ref, wb) * (float(hd) ** -0.5)

    G = g_ref[...] + gb_ref[...]
    lf = jnp.minimum(G, 0.0) - jnp.log(1.0 + jnp.exp(-jnp.abs(G)))
    row = lax.broadcasted_iota(I32, (L, LANES), 0)
    F = lf
    sh = 1
    while sh < L:
        F = F + jnp.where(row >= sh, pltpu.roll(F, sh, axis=0), 0.0)
        sh *= 2
    GT = G.T
    FT = F.T
    ti = lax.broadcasted_iota(I32, (L, L), 0)
    si = lax.broadcasted_iota(I32, (L, L), 1)
    causal = ti >= si
    ones_col = (lax.broadcasted_iota(I32, (L, dp - hd), 1) == 0).astype(BF16)
    ms = ms_ref[...]

    for h in range(nh):
        sl = slice(h * hd, (h + 1) * hd)
        li_c, F_c = G[:, h:h + 1], F[:, nh + h:nh + h + 1]
        li_r, F_r = GT[h:h + 1, :], FT[nh + h:nh + h + 1, :]
        F_last = F_c[L - 1:L, :]
        m_prev = ms[:, h:h + 1]
        a_r = F_last - F_r + li_r
        b = jnp.max(a_r, axis=-1, keepdims=True)
        ea_c = jnp.exp(F_last - F_c + li_c - b)
        logw = jnp.where(causal, F_c - F_r + li_r, NEG)
        m_intra = jnp.max(logw, axis=-1, keepdims=True)
        m_inter = F_c + m_prev
        m_t = jnp.maximum(m_inter, m_intra)
        q = qs[:, sl].astype(BF16)
        k = ks[:, sl]
        v_aug = jnp.concatenate([v_ref[:, sl], ones_col], axis=1)
        s = lax.dot_general(q, k.astype(BF16), (((1,), (1,)), ((), ())),
                            preferred_element_type=F32) * jnp.exp(logw - m_t)
        inter = jnp.exp(m_inter - m_t)
        cn = cn_ref[h]
        num = (jnp.dot(s.astype(BF16), v_aug, preferred_element_type=F32)
               + inter * jnp.dot(q, cn.astype(BF16), preferred_element_type=F32))
        den = num[:, hd:hd + 1]
        hv = num[:, :hd] / jnp.maximum(jnp.abs(den), jnp.exp(-m_t))
        m_new = jnp.maximum(F_last + m_prev, b)
        decay = jnp.exp(F_last + m_prev - m_new)
        inj = jnp.exp(b - m_new)
        kv = jnp.dot((ea_c * k).T.astype(BF16), v_aug, preferred_element_type=F32)
        cn_ref[h] = decay * cn + inj * kv
        ms_ref[:, h:h + 1] = m_new
        hb = _rms(hv, hn_ref[:, sl])
        y_ref[:, sl] = (_sigmoid(ob_ref[:, sl].astype(F32)) * hb).astype(y_ref.dtype)


def mlstm(proj_b, gates, proj_t, conv_w, conv_b3, gbias3, hnorm3, l, cfg):
    S = proj_b.shape[0]
    L, nh, hd = cfg.chunk, cfg.h_b, cfg.hd_b
    wb = nh * hd
    dp = _round_up(hd + 1, LANES)
    assert L == LANES and S % L == 0 and 2 * nh <= LANES and cfg.conv_w <= SUBLANES
    cw = conv_w.shape[1]
    return pl.pallas_call(
        functools.partial(_mlstm_kernel, nh=nh, hd=hd, dp=dp, conv_w=cw),
        grid=(S // L,),
        in_specs=[pl.BlockSpec((L, wb), lambda c: (c, 0)),
                  pl.BlockSpec((L, wb), lambda c: (c, 1)),
                  pl.BlockSpec((L, wb), lambda c: (c, 2)),
                  pl.BlockSpec((L, LANES), lambda c: (c, 0)),
                  pl.BlockSpec((L, wb), lambda c: (c, 0)),
                  pl.BlockSpec((None, cw, 2 * wb), lambda c: (l, 0, 0)),
                  pl.BlockSpec((None, 1, 2 * wb), lambda c: (l, 0, 0)),
                  pl.BlockSpec((None, 1, LANES), lambda c: (l, 0, 0)),
                  pl.BlockSpec((None, 1, wb), lambda c: (l, 0, 0))],
        out_specs=pl.BlockSpec((L, wb), lambda c: (c, 0)),
        out_shape=jax.ShapeDtypeStruct((S, wb), BF16),
        scratch_shapes=[pltpu.VMEM((nh, hd, dp), F32), pltpu.VMEM((1, LANES), F32),
                        pltpu.VMEM((SUBLANES + L, 2 * wb), F32)],
        compiler_params=_cparams(1, 40),
        name="mlstm",
    )(proj_b, proj_b, proj_b, gates, proj_t, conv_w, conv_b3, gbias3, hnorm3)


def _merge_kernel(ya_ref, yb_ref, ym_ref, wa_ref, wb_ref, wm_ref, ga_ref, gb_ref, gm_ref,
                  o_ref, was, wbs, wms):
    @pl.when(pl.program_id(1) == 0)
    def _():
        was[...] = wa_ref[...].astype(BF16)
        wbs[...] = wb_ref[...].astype(BF16)
        wms[...] = wm_ref[...].astype(BF16)

    def branch(y_ref, w_ref, g_ref):
        return _sigmoid(g_ref[...].astype(F32)) * jnp.dot(y_ref[...], w_ref[...],
                                                          preferred_element_type=F32)

    o_ref[...] = (branch(ya_ref, was, ga_ref) + branch(yb_ref, wbs, gb_ref)
                  + branch(ym_ref, wms, gm_ref)).astype(o_ref.dtype)


def gated_merge(ya, yb, ym, w_a, w_b, w_m, proj_t, gate_col0, l, d, tm=1024, tn=512):
    S = ya.shape[0]
    tm = min(tm, S)
    while gate_col0 % tn or d % tn:
        tn //= 2
    goff = gate_col0 // tn
    nd = d // tn

    def lhs_spec(y):
        return pl.BlockSpec((tm, y.shape[1]), lambda n, m: (m, 0))

    def w_spec(w):
        return pl.BlockSpec((None, w.shape[1], tn), lambda n, m: (l, 0, n))

    def g_spec(j):
        return pl.BlockSpec((tm, tn), lambda n, m: (m, goff + j * nd + n))

    ksum = ya.shape[1] + yb.shape[1] + ym.shape[1]
    vmem = 2 * tm * ksum * 2 + 2 * ksum * tn * 4 + ksum * tn * 2 + 8 * tm * tn * 2 + 4 * tm * tn * 4
    return pl.pallas_call(
        _merge_kernel,
        grid=(nd, S // tm),
        in_specs=[lhs_spec(ya), lhs_spec(yb), lhs_spec(ym), w_spec(w_a), w_spec(w_b), w_spec(w_m),
                  g_spec(0), g_spec(1), g_spec(2)],
        out_specs=pl.BlockSpec((tm, tn), lambda n, m: (m, n)),
        out_shape=jax.ShapeDtypeStruct((S, d), BF16),
        scratch_shapes=[pltpu.VMEM((w.shape[1], tn), BF16) for w in (w_a, w_b, w_m)],
        compiler_params=_cparams(2, vmem // MIB + VMEM_SLACK_MIB),
        name="gated_merge",
    )(ya, yb, ym, w_a, w_b, w_m, proj_t, proj_t, proj_t)


def _router_kernel(x_ref, g_ref, wr_ref, br_ref, h_ref, meta_ref, gate_ref, cnt_ref, run_ref, *, n_exp):
    i = pl.program_id(0)
    tm = x_ref.shape[0]

    @pl.when(i == 0)
    def _():
        run_ref[...] = jnp.zeros_like(run_ref)

    y = _rms(x_ref[...], g_ref[...])
    h_ref[...] = y
    logits = jnp.dot(y, wr_ref[...], preferred_element_type=F32,
                     precision=lax.Precision.HIGHEST) + br_ref[...]
    lane = lax.broadcasted_iota(I32, (tm, LANES), 1)
    lanef = lane.astype(F32)
    logits = jnp.where(lane < n_exp, logits, -jnp.inf)
    v1 = jnp.max(logits, axis=-1, keepdims=True)
    i1 = jnp.min(jnp.where(logits == v1, lanef, float(LANES)), axis=-1, keepdims=True).astype(I32)
    rest = jnp.where(lane == i1, -jnp.inf, logits)
    v2 = jnp.max(rest, axis=-1, keepdims=True)
    i2 = jnp.min(jnp.where(rest == v2, lanef, float(LANES)), axis=-1, keepdims=True).astype(I32)
    e = jnp.exp(v2 - v1)
    g1 = 1.0 / (1.0 + e)
    g2 = e / (1.0 + e)
    oh1 = lane == i1
    oh2 = lane == i2
    oh = jnp.where(oh1 | oh2, 1.0, 0.0)
    r = lax.broadcasted_iota(I32, (tm, tm), 0)
    cidx = lax.broadcasted_iota(I32, (tm, tm), 1)
    tri = jnp.where(cidx < r, 1.0, 0.0).astype(BF16)
    cum = jnp.dot(tri, oh.astype(BF16), preferred_element_type=F32) + run_ref[...]
    r1 = jnp.sum(jnp.where(oh1, cum, 0.0), axis=-1, keepdims=True).astype(I32)
    r2 = jnp.sum(jnp.where(oh2, cum, 0.0), axis=-1, keepdims=True).astype(I32)
    run_ref[...] += jnp.sum(oh, axis=0, keepdims=True)
    meta_ref[...] = jnp.where(lane == 0, i1, jnp.where(lane == 1, i2,
                              jnp.where(lane == 2, r1, jnp.where(lane == 3, r2, 0))))
    gate_ref[...] = jnp.where(lane == 0, g1, jnp.where(lane == 1, g2, 0.0))
    cnt_ref[...] = run_ref[...]


def norm_router(x, g3, w_router_p, b_router_p, l, lr, n_exp, tm=256):
    S, D = x.shape
    tm = min(tm, S)
    return pl.pallas_call(
        functools.partial(_router_kernel, n_exp=n_exp),
        grid=(S // tm,),
        in_specs=[pl.BlockSpec((tm, D), lambda i: (i, 0)),
                  pl.BlockSpec((None, 1, D), lambda i: (l, 0, 0)),
                  pl.BlockSpec((None, D, LANES), lambda i: (lr, 0, 0)),
                  pl.BlockSpec((None, 1, LANES), lambda i: (lr, 0, 0))],
        out_specs=[pl.BlockSpec((tm, D), lambda i: (i, 0)),
                   pl.BlockSpec((tm, LANES), lambda i: (i, 0)),
                   pl.BlockSpec((tm, LANES), lambda i: (i, 0)),
                   pl.BlockSpec((1, LANES), lambda i: (0, 0))],
        out_shape=[jax.ShapeDtypeStruct((S, D), F32), jax.ShapeDtypeStruct((S, LANES), I32),
                   jax.ShapeDtypeStruct((S, LANES), F32), jax.ShapeDtypeStruct((1, LANES), F32)],
        scratch_shapes=[pltpu.VMEM((1, LANES), F32)],
        compiler_params=_cparams(1, 40),
        name="norm_router",
    )(x, g3, w_router_p, b_router_p)


def _row_copy(src_hbm, dst, src_row, dst_row, sem):
    return pltpu.make_async_copy(src_hbm.at[pl.ds(src_row, 1)], dst.at[pl.ds(dst_row, 1)], sem)


def _gather_kernel(order_ref, nv_ref, h_hbm, xs_ref, buf, sem, *, tg):
    i = pl.program_id(0)
    nv = nv_ref[0]
    slot = i % 2

    def start_tile(t, s):
        def body(it, carry):
            for q in range(2):
                r = 2 * it + q
                _row_copy(h_hbm, buf.at[s], order_ref[t * tg + r], r, sem.at[s]).start(priority=q)
            return carry
        lax.fori_loop(0, tg // 2, body, 0, unroll=4)

    def wait_tile(s):
        def body(r, carry):
            _row_copy(h_hbm, buf.at[s], 0, r, sem.at[s]).wait()
            return carry
        lax.fori_loop(0, tg, body, 0, unroll=8)

    @pl.when(i == 0)
    def _():
        start_tile(0, 0)

    @pl.when(i + 1 < nv)
    def _():
        start_tile(i + 1, 1 - slot)

    @pl.when(i < nv)
    def _():
        wait_tile(slot)
        xs_ref[...] = buf[slot].astype(BF16)

    @pl.when(i >= nv)
    def _():
        xs_ref[...] = jnp.zeros_like(xs_ref)


def gather_rows(h, order, nvalid, n_tiles, tg):
    D = h.shape[1]
    grid_spec = pltpu.PrefetchScalarGridSpec(
        num_scalar_prefetch=2,
        grid=(n_tiles,),
        in_specs=[pl.BlockSpec(memory_space=pl.ANY)],
        out_specs=pl.BlockSpec((tg, D), lambda i, order, nv: (i, 0)),
        scratch_shapes=[pltpu.VMEM((2, tg, D), F32), pltpu.SemaphoreType.DMA((2,))],
    )
    return pl.pallas_call(
        functools.partial(_gather_kernel, tg=tg),
        grid_spec=grid_spec,
        out_shape=jax.ShapeDtypeStruct((n_tiles * tg, D), BF16),
        compiler_params=_cparams(1, 40),
        name="moe_gather",
    )(order, nvalid, h)


def _gmm_kernel(te_ref, src_ref, first_ref, nxt_ref, lastrun_ref, nv_ref, *refs, nw, epi, lr, tn):
    x_ref = refs[0]
    w_hbm = refs[1:1 + nw]
    out_ref = refs[1 + nw]
    stage, wbf, sem = refs[2 + nw:5 + nw]
    n = pl.program_id(0)
    i = pl.program_id(1)

    def w_copy(j, e, nb):
        return pltpu.make_async_copy(w_hbm[j].at[lr, e, :, pl.ds(nb * tn, tn)], stage.at[j], sem.at[j])

    @pl.when((n == 0) & (i == 0))
    def _():
        for j in range(nw):
            w_copy(j, te_ref[0], 0).start()

    @pl.when(first_ref[i] == 1)
    def _():
        for j in range(nw):
            w_copy(j, te_ref[i], n).wait()
        _cast_blocks(stage, wbf, nw)
        nb_next = n + lastrun_ref[i]

        @pl.when(nb_next < pl.num_programs(0))
        def _():
            for j in range(nw):
                w_copy(j, nxt_ref[i], nb_next).start()

    @pl.when(i < nv_ref[0])
    def _():
        x = x_ref[...]
        accs = [jnp.dot(x, wbf[j], preferred_element_type=F32) for j in range(nw)]
        if epi == "swiglu":
            g, u = accs
            out = g * _sigmoid(g) * u
        else:
            out = accs[0]
        out_ref[...] = out.astype(out_ref.dtype)

    @pl.when(i >= nv_ref[0])
    def _():
        out_ref[...] = jnp.zeros_like(out_ref)


def grouped_matmul(xs, ws, lr, tile_meta, *, epi, out_dtype, tg, tn, name):
    P, K = xs.shape
    N = ws[0].shape[-1]
    nw = len(ws)
    tn = min(tn, N)
    assert N % tn == 0 and P % tg == 0
    grid_spec = pltpu.PrefetchScalarGridSpec(
        num_scalar_prefetch=len(tile_meta),
        grid=(N // tn, P // tg),
        in_specs=[pl.BlockSpec((tg, K), lambda n, i, te, src, *_: (src[i], 0))]
        + [pl.BlockSpec(memory_space=pl.ANY)] * nw,
        out_specs=pl.BlockSpec((tg, tn), lambda n, i, *_: (i, n)),
        scratch_shapes=[pltpu.VMEM((nw, K, tn), F32), pltpu.VMEM((nw, K, tn), BF16),
                        pltpu.SemaphoreType.DMA((nw,))],
    )
    obytes = jnp.dtype(out_dtype).itemsize
    vmem = 2 * tg * K * 2 + nw * K * tn * 6 + 2 * tg * tn * obytes + (nw + 1) * tg * tn * 4
    return pl.pallas_call(
        functools.partial(_gmm_kernel, nw=nw, epi=epi, lr=lr, tn=tn),
        grid_spec=grid_spec,
        out_shape=jax.ShapeDtypeStruct((P, N), out_dtype),
        compiler_params=_cparams(2, vmem // MIB + VMEM_SLACK_MIB),
        name=name,
    )(*tile_meta, xs, *ws)


def _combine_kernel(dest_ref, x_ref, gate_ref, ys_hbm, out_ref, buf, sem, *, tc, top_k):
    i = pl.program_id(0)
    slot = i % 2

    def start_tile(t, s):
        def body(r, carry):
            for k in range(top_k):
                _row_copy(ys_hbm, buf.at[s, k], dest_ref[(t * tc + r) * top_k + k], r,
                          sem.at[s]).start(priority=k % 2)
            return carry
        lax.fori_loop(0, tc, body, 0, unroll=4)

    def wait_tile(s):
        def body(r, carry):
            for k in range(top_k):
                _row_copy(ys_hbm, buf.at[s, k], 0, r, sem.at[s]).wait()
            return carry
        lax.fori_loop(0, tc, body, 0, unroll=4)

    @pl.when(i == 0)
    def _():
        start_tile(0, 0)

    @pl.when(i + 1 < pl.num_programs(0))
    def _():
        start_tile(i + 1, 1 - slot)

    wait_tile(slot)
    g = gate_ref[...]
    out = x_ref[...]
    for k in range(top_k):
        out = out + g[:, k:k + 1] * buf[slot, k]
    out_ref[...] = out


def moe_combine(x, gates, ys, dest_flat, top_k, tc=256):
    S, D = x.shape
    tc = min(tc, S)
    grid_spec = pltpu.PrefetchScalarGridSpec(
        num_scalar_prefetch=1,
        grid=(S // tc,),
        in_specs=[pl.BlockSpec((tc, D), lambda i, d: (i, 0)),
                  pl.BlockSpec((tc, LANES), lambda i, d: (i, 0)),
                  pl.BlockSpec(memory_space=pl.ANY)],
        out_specs=pl.BlockSpec((tc, D), lambda i, d: (i, 0)),
        scratch_shapes=[pltpu.VMEM((2, top_k, tc, D), F32), pltpu.SemaphoreType.DMA((2,))],
    )
    return pl.pallas_call(
        functools.partial(_combine_kernel, tc=tc, top_k=top_k),
        grid_spec=grid_spec,
        out_shape=jax.ShapeDtypeStruct((S, D), F32),
        compiler_params=_cparams(1, 48),
        name="moe_combine",
    )(dest_flat, x, gates, ys)


def moe_layer(x, norm_ffn3, l, w_router, b_router, w_e_gate, w_e_up, w_e_down, lr, cfg):
    S, D = x.shape
    E, top_k, tg = cfg.n_experts, 2, cfg.tg
    tg = min(tg, S)
    wr = jnp.pad(w_router, ((0, 0), (0, 0), (0, LANES - E)))
    br = jnp.pad(b_router, ((0, 0), (0, LANES - E)))[:, None, :]
    h, meta, gates, cnt = norm_router(x, norm_ffn3, wr, br, l, lr, E)

    eid, rank = meta[:, 0:top_k], meta[:, top_k:2 * top_k]
    counts = cnt[0, :E].astype(I32)
    padded = (counts + tg - 1) // tg * tg
    ends = jnp.cumsum(padded)
    dest = (ends - padded)[eid] + rank
    n_tiles = (S * top_k) // tg + E
    order = jnp.zeros((n_tiles * tg,), I32).at[dest.reshape(-1)].set(
        jnp.repeat(jnp.arange(S, dtype=I32), top_k))
    nvalid = (ends[-1] // tg).astype(I32)
    tile = jnp.arange(n_tiles, dtype=I32)
    src = jnp.minimum(tile, nvalid - 1)
    te = jnp.sum((src * tg)[:, None] >= ends[None, :], axis=1).astype(I32)
    first = ((tile == 0) | (te != jnp.roll(te, 1))).astype(I32)
    later = (tile[None, :] > tile[:, None]) & (te[None, :] != te[:, None])
    nxt_idx = jnp.min(jnp.where(later, tile[None, :], n_tiles), axis=1)
    lastrun = (nxt_idx == n_tiles).astype(I32)
    nxt = jnp.where(lastrun == 1, te[0], te[jnp.minimum(nxt_idx, n_tiles - 1)]).astype(I32)
    nv = nvalid.reshape(1)
    tile_meta = (te, src, first, nxt, lastrun, nv)

    xs = gather_rows(h, order, nv, n_tiles, tg)
    a = grouped_matmul(xs, [w_e_gate, w_e_up], lr, tile_meta, epi="swiglu", out_dtype=BF16,
                       tg=tg, tn=cfg.tn_b, name="moe_gate_up")
    ys = grouped_matmul(a, [w_e_down], lr, tile_meta, epi="plain", out_dtype=F32,
                        tg=tg, tn=cfg.tn_gd, name="moe_down")
    return moe_combine(x, gates, ys, dest.reshape(-1), top_k)


def _forward(x, mem, rel_bias, norm_mix, norm_ffn, norm_mem, w_in, qn_a, kn_a, conv_w, conv_b,
             gate_bias_b, hnorm_b, w_mem_kv, qn_m, kn_m, w_br_a, w_br_b, w_br_m, w_out,
             w_ff_gate, w_ff_up, w_ff_down, w_router, b_router, w_e_gate, w_e_up, w_e_down, cfg):
    B, S, D = x.shape
    assert B == 1 and mem.shape[0] == 1
    depth = norm_mix.shape[0]
    ng = len(cfg.a_groups)
    w_a = ng * cfg.heads_per_group * cfg.hd_a
    w_b = cfg.h_b * cfg.hd_b
    w_m = cfg.h_m * cfg.hd_m
    if_col0 = 3 * w_a + 3 * w_b
    tail_col0 = if_col0 + 2 * cfg.h_b
    d_ff = w_ff_gate.shape[-1]
    tk_down = min(cfg.tk_down, _round_up(d_ff, LANES))
    d_ff_p = _round_up(d_ff, tk_down)

    def row3(p):
        return p[:, None, :]

    x = x.reshape(S, D)
    mem2 = mem.reshape(mem.shape[1], D)
    norm_mix3, norm_ffn3, norm_mem3 = row3(norm_mix), row3(norm_ffn), row3(norm_mem)
    qn_a3, kn_a3, qn_m3, kn_m3 = row3(qn_a), row3(kn_a), row3(qn_m), row3(kn_m)
    conv_b3, hnorm3 = row3(conv_b), row3(hnorm_b)
    gbias3 = row3(jnp.pad(gate_bias_b, ((0, 0), (0, LANES - 2 * cfg.h_b))))
    mm = functools.partial(matmul, tm=cfg.tm)
    gw = cfg.heads_per_group * cfg.hd_a
    w_in_t = jnp.swapaxes(w_in, 1, 2)

    for l in range(depth):
        h = rmsnorm(x, norm_mix3, l)
        proj_g = [mm(h, [w_in_t], lead=l, trans=True, n_out=3 * gw, tn=gw,
                     blk_of=lambda n, g=g: n * ng + g, name=f"proj_a{g}")
                  for g in range(ng)]
        proj_b = mm(h, [w_in_t], lead=l, trans=True, col0=3 * w_a, n_out=3 * w_b, tn=cfg.tn_b,
                    name="proj_b")
        gates_b = mm(h, [w_in_t], lead=l, trans=True, col0=if_col0, n_out=LANES, out_dtype=F32,
                     tn=LANES, name="proj_if")
        proj_t = mm(h, [w_in_t], lead=l, trans=True, col0=tail_col0, n_out=w_b + w_m, tn=cfg.tn_b,
                    name="proj_tail")
        proj_gt = mm(h, [w_in_t], lead=l, trans=True, col0=tail_col0 + w_b + w_m, n_out=3 * D,
                     tn=cfg.tn, name="proj_gates")

        outs, lses = zip(*[band_attention(proj_g[g], rel_bias, qn_a3, kn_a3, l, g, cfg)
                           for g in range(ng)])
        y_a = alpha_merge(outs, lses)
        y_b = mlstm(proj_b, gates_b, proj_t, conv_w, conv_b3, gbias3, hnorm3, l, cfg)
        hm = rmsnorm(mem2, norm_mem3, l)
        kv = mm(hm, [w_mem_kv], lead=l, n_out=2 * w_m, tn=cfg.tn, name="mem_kv")
        y_m = cross_attention(proj_t, w_b, kv, qn_m3, kn_m3, l, cfg)
        y = gated_merge(y_a, y_b, y_m, w_br_a, w_br_b, w_br_m, proj_gt, 0, l, D,
                        tm=cfg.tm // 2, tn=cfg.tn_b)
        x = mm(y, [w_out], lead=l, n_out=D, epi="residual", res=x, out_dtype=F32, tn=cfg.tn_b,
               name="out_proj")

        if l % 2 == 0:
            ld = l // 2
            h2 = rmsnorm(x, norm_ffn3, l)
            a = matmul(h2, [w_ff_gate, w_ff_up], lead=ld, n_out=d_ff_p, epi="swiglu", n_valid=d_ff,
                       tm=cfg.tm // 2, tn=cfg.tn2, name="ffn_gate_up")
            wd = jnp.pad(w_ff_down[ld], ((0, d_ff_p - d_ff), (0, 0))).astype(BF16)
            x = matmul_ktiled_residual(a, wd, x, tm=cfg.tm, tn=cfg.tn_down, tk=tk_down, name="ffn_down")
        else:
            x = moe_layer(x, norm_ffn3, l, w_router, b_router, w_e_gate, w_e_up, w_e_down, l // 2, cfg)
    return x.reshape(B, S, D)


def kernel(x, mem, rel_bias, norm_mix, norm_ffn, norm_mem, w_in, qn_a, kn_a, conv_w, conv_b, gate_bias_b, hnorm_b, w_mem_kv, qn_m, kn_m, w_br_a, w_br_b, w_br_m, w_out, w_ff_gate, w_ff_up, w_ff_down, w_router, b_router, w_e_gate, w_e_up, w_e_down):
    return _forward(x, mem, rel_bias, norm_mix, norm_ffn, norm_mem, w_in, qn_a, kn_a, conv_w, conv_b,
                    gate_bias_b, hnorm_b, w_mem_kv, qn_m, kn_m, w_br_a, w_br_b, w_br_m, w_out,
                    w_ff_gate, w_ff_up, w_ff_down, w_router, b_router, w_e_gate, w_e_up, w_e_down,
                    Cfg())
```

```python
import functools
from typing import NamedTuple

import numpy as np
import jax
import jax.numpy as jnp
from jax import lax
from jax.experimental import pallas as pl
from jax.experimental.pallas import tpu as pltpu

F32 = jnp.float32
BF16 = jnp.bfloat16
I32 = jnp.int32
EPS = 1e-6
NEG = -1e30
MIB = 1 << 20
LANES = 128
SUBLANES = 8
VMEM_CAP_MIB = 60
VMEM_SLACK_MIB = 12


class Cfg(NamedTuple):
    a_groups: tuple = ((128, 1), (512, 4), (2048, 16))
    heads_per_group: int = 4
    hd_a: int = 128
    band_block: int = 128
    h_b: int = 4
    hd_b: int = 384
    chunk: int = 128
    conv_w: int = 4
    h_m: int = 4
    hd_m: int = 256
    n_buckets: int = 32
    max_dist: int = 2048
    n_experts: int = 8
    tm: int = 1024
    tn: int = 1024
    tn_b: int = 512
    tn2: int = 512
    tn_gd: int = 1024
    tg: int = 512
    tk_down: int = 5504
    tn_down: int = 512


def _cparams(n_axes, vmem_mib):
    return pltpu.CompilerParams(dimension_semantics=("arbitrary",) * n_axes,
                                vmem_limit_bytes=int(min(vmem_mib, VMEM_CAP_MIB)) * MIB)


def _round_up(a, b):
    return -(-a // b) * b


def _sigmoid(x):
    return 1.0 / (1.0 + jnp.exp(-x))


def _rms(x, g):
    return x * lax.rsqrt(jnp.mean(x * x, axis=-1, keepdims=True) + EPS) * g


def _rmsnorm_kernel(x_ref, g_ref, o_ref):
    o_ref[...] = _rms(x_ref[...].astype(F32), g_ref[...]).astype(o_ref.dtype)


def rmsnorm(x, g3, l, out_dtype=BF16, tm=256):
    M, D = x.shape
    tm = min(tm, M)
    return pl.pallas_call(
        _rmsnorm_kernel,
        grid=(M // tm,),
        in_specs=[pl.BlockSpec((tm, D), lambda i: (i, 0)),
                  pl.BlockSpec((None, 1, D), lambda i: (l, 0, 0))],
        out_specs=pl.BlockSpec((tm, D), lambda i: (i, 0)),
        out_shape=jax.ShapeDtypeStruct((M, D), out_dtype),
        compiler_params=_cparams(1, 32),
        name="rmsnorm",
    )(x, g3)


def _cast_blocks(stage, wbf, nw, rows=512):
    n_rows = stage.shape[1]
    for j in range(nw):
        for r in range(0, n_rows, rows):
            sl = slice(r, min(r + rows, n_rows))
            wbf[j, sl] = stage[j, sl].astype(BF16)


def _mm_kernel(lhs_ref, *refs, nw, trans, epi, n_valid, tn, gn, last_w, src_of):
    w_hbm = refs[:nw]
    pos = nw
    res_ref = None
    if epi == "residual":
        res_ref = refs[pos]
        pos += 1
    out_ref = refs[pos]
    stage, wbf, sem = refs[pos + 1:pos + 4]
    n = pl.program_id(0)
    m = pl.program_id(1)

    def copies(nb, width):
        out = []
        for j in range(nw):
            if width == tn:
                dst = stage.at[j]
            elif trans:
                dst = stage.at[j, pl.ds(0, width), :]
            else:
                dst = stage.at[j, :, pl.ds(0, width)]
            out.append(pltpu.make_async_copy(src_of(w_hbm[j], nb, width), dst, sem.at[j]))
        return out

    def for_block(nb, fn):
        if last_w == tn:
            for c in copies(nb, tn):
                fn(c)
        elif isinstance(nb, int):
            for c in copies(nb, last_w if nb == gn - 1 else tn):
                fn(c)
        else:
            @pl.when(nb == gn - 1)
            def _():
                for c in copies(nb, last_w):
                    fn(c)

            @pl.when(nb != gn - 1)
            def _():
                for c in copies(nb, tn):
                    fn(c)

    @pl.when((n == 0) & (m == 0))
    def _():
        for_block(0, lambda c: c.start())

    @pl.when(m == 0)
    def _():
        for_block(n, lambda c: c.wait())
        _cast_blocks(stage, wbf, nw)

        @pl.when(n + 1 < gn)
        def _():
            for_block(n + 1, lambda c: c.start())

    lhs = lhs_ref[...]
    dims = (((1,), (1,)), ((), ())) if trans else (((1,), (0,)), ((), ()))
    accs = [lax.dot_general(lhs, wbf[j], dims, preferred_element_type=F32) for j in range(nw)]
    if epi == "plain":
        out = accs[0]
    elif epi == "residual":
        out = res_ref[...] + accs[0]
    else:
        g, u = accs
        out = g * _sigmoid(g) * u
        if n_valid is not None:
            col = pl.program_id(0) * tn + lax.broadcasted_iota(I32, out.shape, 1)
            out = jnp.where(col < n_valid, out, 0.0)
    out_ref[...] = out.astype(out_ref.dtype)


def matmul(lhs, ws, *, lead=None, col0=0, n_out, epi="plain", res=None, n_valid=None,
           trans=False, blk_of=None, out_dtype=BF16, tm=1024, tn=512, name="matmul"):
    M, K = lhs.shape
    nw = len(ws)
    tm = min(tm, M)
    tn = min(tn, _round_up(n_out, LANES))
    assert tn % LANES == 0 and M % tm == 0 and ws[0].dtype == F32
    gn = pl.cdiv(n_out, tn)
    last_w = n_out - (gn - 1) * tn
    if blk_of is None:
        def col_of(nb):
            return col0 + nb * tn
    else:
        def col_of(nb):
            return blk_of(nb) * tn
    if trans:
        n_rows = ws[0].shape[1]
        assert lead is not None and n_rows % SUBLANES == 0 and col0 % SUBLANES == 0 and last_w % SUBLANES == 0
        ws = [w.reshape(-1, K) for w in ws]
        w_block = (tn, K)

        def src_of(w, nb, width):
            return w.at[pl.ds(pl.multiple_of(lead * n_rows + col_of(nb), SUBLANES), width), :]
    else:
        n_cols = ws[0].shape[-1]
        assert col0 % LANES == 0 and last_w % LANES == 0 and n_cols % LANES == 0
        assert col0 + n_out <= n_cols or n_valid is not None
        w_block = (K, tn)

        def src_of(w, nb, width):
            w = w if lead is None else w.at[lead]
            col = jnp.minimum(col_of(nb), n_cols - width)
            return w.at[:, pl.ds(pl.multiple_of(col, LANES), width)]
    in_specs = [pl.BlockSpec((tm, K), lambda n, m: (m, 0))] + [pl.BlockSpec(memory_space=pl.ANY)] * nw
    args = [lhs] + list(ws)
    if epi == "residual":
        in_specs.append(pl.BlockSpec((tm, tn), lambda n, m: (m, n)))
        args.append(res)
    obytes = jnp.dtype(out_dtype).itemsize
    vmem = (3 * tm * K * 2 + nw * K * tn * 6 + 2 * tm * tn * obytes
            + (2 * tm * tn * 4 if epi == "residual" else 0) + (nw + 1) * tm * tn * 4)
    return pl.pallas_call(
        functools.partial(_mm_kernel, nw=nw, trans=trans, epi=epi, n_valid=n_valid, tn=tn, gn=gn,
                          last_w=last_w, src_of=src_of),
        grid=(gn, M // tm),
        in_specs=in_specs,
        out_specs=pl.BlockSpec((tm, tn), lambda n, m: (m, n)),
        out_shape=jax.ShapeDtypeStruct((M, n_out), out_dtype),
        scratch_shapes=[pltpu.VMEM((nw,) + w_block, F32), pltpu.VMEM((nw,) + w_block, BF16),
                        pltpu.SemaphoreType.DMA((nw,))],
        compiler_params=_cparams(2, vmem // MIB + VMEM_SLACK_MIB),
        name=name,
    )(*args)


def _mmk_kernel(lhs_ref, w_ref, res_ref, out_ref, acc_ref):
    k = pl.program_id(2)

    @pl.when(k == 0)
    def _():
        acc_ref[...] = jnp.zeros_like(acc_ref)

    acc_ref[...] += jnp.dot(lhs_ref[...], w_ref[...], preferred_element_type=F32)

    @pl.when(k == pl.num_programs(2) - 1)
    def _():
        out_ref[...] = res_ref[...] + acc_ref[...]


def matmul_ktiled_residual(lhs, w, res, *, tm=1024, tn=1024, tk=1024, name="matmul_k"):
    M, K = lhs.shape
    N = w.shape[1]
    tm, tn, tk = min(tm, M), min(tn, N), min(tk, K)
    assert M % tm == 0 and N % tn == 0 and K % tk == 0
    vmem = 2 * tm * tk * 2 + 2 * tk * tn * 2 + 5 * tm * tn * 4
    return pl.pallas_call(
        _mmk_kernel,
        grid=(M // tm, N // tn, K // tk),
        in_specs=[pl.BlockSpec((tm, tk), lambda m, n, k: (m, k)),
                  pl.BlockSpec((tk, tn), lambda m, n, k: (k, n)),
                  pl.BlockSpec((tm, tn), lambda m, n, k: (m, n))],
        out_specs=pl.BlockSpec((tm, tn), lambda m, n, k: (m, n)),
        out_shape=jax.ShapeDtypeStruct((M, N), F32),
        scratch_shapes=[pltpu.VMEM((tm, tn), F32)],
        compiler_params=_cparams(3, vmem // MIB + VMEM_SLACK_MIB),
        name=name,
    )(lhs, w, res)


def _t5_bucket_np(dist, n_buckets, max_dist):
    max_exact = n_buckets // 2
    d = np.maximum(dist, 1).astype(np.float32)
    large = max_exact + (np.log(d / np.float32(max_exact)) / np.float32(np.log(max_dist / max_exact))
                         * np.float32(n_buckets - max_exact)).astype(np.int32)
    large = np.minimum(large, n_buckets - 1)
    return np.where(dist < max_exact, dist, large).astype(np.int32)


def _band_kernel(tab_ref, bkt_ref, q_ref, kp_ref, kc_ref, vp_ref, vc_ref, qn_ref, kn_ref,
                 o_ref, lse_ref, bias_ref, *, hpg, hd, head0, buckets, scale):
    p = pl.program_id(0)
    n = pl.program_id(1)
    bb = q_ref.shape[0]

    @pl.when((p == 0) & (n == 0))
    def _():
        bkt = bkt_ref[...]
        for h in range(hpg):
            acc = jnp.full((bb, 2 * bb), NEG, F32)
            for b in buckets:
                acc = jnp.where(bkt == b, tab_ref[b, head0 + h], acc)
            bias_ref[h] = acc

    ki = lax.broadcasted_iota(I32, (bb, 2 * bb), 1)
    kvalid = (ki >= bb) | (n > 0)
    for h in range(hpg):
        sl = slice(h * hd, (h + 1) * hd)
        q = _rms(q_ref[:, sl].astype(F32), qn_ref[...]) * scale
        k = jnp.concatenate([kp_ref[:, sl], kc_ref[:, sl]], axis=0).astype(F32)
        k = _rms(k, kn_ref[...])
        v = jnp.concatenate([vp_ref[:, sl], vc_ref[:, sl]], axis=0)
        s = lax.dot_general(q.astype(BF16), k.astype(BF16), (((1,), (1,)), ((), ())),
                            preferred_element_type=F32)
        s = jnp.where(kvalid, s + bias_ref[h], NEG)
        m = jnp.max(s, axis=-1, keepdims=True)
        e = jnp.exp(s - m)
        l = jnp.sum(e, axis=-1, keepdims=True)
        o = jnp.dot((e / l).astype(BF16), v, preferred_element_type=F32)
        o_ref[:, sl] = o.astype(o_ref.dtype)
        lse_ref[:, sl] = jnp.broadcast_to(m + jnp.log(l), (bb, hd))


def band_attention(proj_g, rel_bias, qn3, kn3, l, g, cfg):
    S, npa = proj_g.shape
    win, dil = cfg.a_groups[g]
    steps = win // dil
    bb, hpg, hd = cfg.band_block, cfg.heads_per_group, cfg.hd_a
    gw = hpg * hd
    ls = S // dil
    assert S % dil == 0 and ls % bb == 0 and steps <= bb and npa == 3 * gw
    nblk = ls // bb
    nb = 3
    x = proj_g.reshape(ls, dil * npa)

    qi = np.arange(bb)[:, None]
    ki = np.arange(2 * bb)[None, :]
    rel = qi + bb - ki
    inside = (rel >= 0) & (rel <= steps)
    bkt = np.where(inside, _t5_bucket_np(np.maximum(rel, 0) * dil, cfg.n_buckets, cfg.max_dist), -1)
    buckets = tuple(int(b) for b in np.unique(bkt[inside]))

    def qmap(p, n):
        return (n, p * nb)

    def kmap_c(p, n):
        return (n, p * nb + 1)

    def kmap_p(p, n):
        return (jnp.maximum(n - 1, 0), p * nb + 1)

    def vmap_c(p, n):
        return (n, p * nb + 2)

    def vmap_p(p, n):
        return (jnp.maximum(n - 1, 0), p * nb + 2)

    blk = (bb, gw)
    o, lse = pl.pallas_call(
        functools.partial(_band_kernel, hpg=hpg, hd=hd, head0=g * hpg, buckets=buckets,
                          scale=float(hd) ** -0.5),
        grid=(dil, nblk),
        in_specs=[pl.BlockSpec(memory_space=pltpu.SMEM),
                  pl.BlockSpec((bb, 2 * bb), lambda p, n: (0, 0)),
                  pl.BlockSpec(blk, qmap), pl.BlockSpec(blk, kmap_p), pl.BlockSpec(blk, kmap_c),
                  pl.BlockSpec(blk, vmap_p), pl.BlockSpec(blk, vmap_c),
                  pl.BlockSpec((None, 1, hd), lambda p, n: (l, 0, 0)),
                  pl.BlockSpec((None, 1, hd), lambda p, n: (l, 0, 0))],
        out_specs=[pl.BlockSpec(blk, lambda p, n: (n, p)), pl.BlockSpec(blk, lambda p, n: (n, p))],
        out_shape=[jax.ShapeDtypeStruct((ls, dil * gw), BF16),
                   jax.ShapeDtypeStruct((ls, dil * gw), F32)],
        scratch_shapes=[pltpu.VMEM((hpg, bb, 2 * bb), F32)],
        compiler_params=_cparams(2, 32),
        name=f"band_attn_g{g}",
    )(rel_bias, jnp.asarray(bkt, I32), x, x, x, x, x, qn3, kn3)
    return o.reshape(S, gw), lse.reshape(S, gw)


def _alpha_kernel(*refs, ng, gw):
    o_refs, l_refs, y_ref = refs[:ng], refs[ng:2 * ng], refs[2 * ng]
    ls = [r[...] for r in l_refs]
    m = functools.reduce(jnp.maximum, ls)
    es = [jnp.exp(v - m) for v in ls]
    den = functools.reduce(lambda a, b: a + b, es)
    for g in range(ng):
        y_ref[:, g * gw:(g + 1) * gw] = (es[g] / den * o_refs[g][...].astype(F32)).astype(y_ref.dtype)


def alpha_merge(outs, lses, tm=512):
    ng = len(outs)
    S, gw = outs[0].shape
    tm = min(tm, S)
    spec = pl.BlockSpec((tm, gw), lambda i: (i, 0))
    return pl.pallas_call(
        functools.partial(_alpha_kernel, ng=ng, gw=gw),
        grid=(S // tm,),
        in_specs=[spec] * (2 * ng),
        out_specs=pl.BlockSpec((tm, ng * gw), lambda i: (i, 0)),
        out_shape=jax.ShapeDtypeStruct((S, ng * gw), BF16),
        compiler_params=_cparams(1, 32),
        name="alpha_merge",
    )(*outs, *lses)


def _cross_kernel(q_ref, k_ref, v_ref, qn_ref, kn_ref, o_ref, *, scale):
    q = _rms(q_ref[...].astype(F32), qn_ref[...]) * scale
    k = _rms(k_ref[...].astype(F32), kn_ref[...])
    s = lax.dot_general(q.astype(BF16), k.astype(BF16), (((1,), (1,)), ((), ())),
                        preferred_element_type=F32)
    m = jnp.max(s, axis=-1, keepdims=True)
    e = jnp.exp(s - m)
    pr = e / jnp.sum(e, axis=-1, keepdims=True)
    o_ref[...] = jnp.dot(pr.astype(BF16), v_ref[...], preferred_element_type=F32).astype(o_ref.dtype)


def cross_attention(proj_t, q_col0, kv, qn3, kn3, l, cfg, tm=1024):
    S = proj_t.shape[0]
    mlen = kv.shape[0]
    hm, hd = cfg.h_m, cfg.hd_m
    tm = min(tm, S)
    assert q_col0 % hd == 0
    qoff = q_col0 // hd
    return pl.pallas_call(
        functools.partial(_cross_kernel, scale=float(hd) ** -0.5),
        grid=(S // tm, hm),
        in_specs=[pl.BlockSpec((tm, hd), lambda i, h: (i, qoff + h)),
                  pl.BlockSpec((mlen, hd), lambda i, h: (0, h)),
                  pl.BlockSpec((mlen, hd), lambda i, h: (0, hm + h)),
                  pl.BlockSpec((None, 1, hd), lambda i, h: (l, 0, 0)),
                  pl.BlockSpec((None, 1, hd), lambda i, h: (l, 0, 0))],
        out_specs=pl.BlockSpec((tm, hd), lambda i, h: (i, h)),
        out_shape=jax.ShapeDtypeStruct((S, hm * hd), BF16),
        compiler_params=_cparams(2, 32),
        name="cross_attn",
    )(proj_t, kv, kv, qn3, kn3)


def _mlstm_kernel(q_ref, k_ref, v_ref, g_ref, ob_ref, cw_ref, cb_ref, gb_ref, hn_ref,
                  y_ref, cn_ref, ms_ref, xw_ref, *, nh, hd, dp, conv_w):
    c = pl.program_id(0)
    L = q_ref.shape[0]
    wb = nh * hd

    @pl.when(c == 0)
    def _():
        cn_ref[...] = jnp.zeros_like(cn_ref)
        ms_ref[...] = jnp.zeros_like(ms_ref)
        xw_ref[:SUBLANES] = jnp.zeros((SUBLANES, 2 * wb), F32)

    def conv_silu(x_ref, col0):
        cols = slice(col0, col0 + wb)
        x = x_ref[...].astype(F32)
        xw_ref[SUBLANES:, cols] = x
        y = cb_ref[:, cols] + cw_ref[conv_w - 1:conv_w, cols] * x
        for s in range(1, conv_w):
            y = y + cw_ref[conv_w - 1 - s:conv_w - s, cols] * xw_ref[SUBLANES - s:SUBLANES - s + L, cols]
        xw_ref[:SUBLANES, cols] = x[L - SUBLANES:]
        return y * _sigmoid(y)

    qs = conv_silu(q_ref, 0)
    ks = conv_silu(k_ref, wb) * (float(hd) ** -0.5)

    G = g_ref[...] + gb_ref[...]
    lf = jnp.minimum(G, 0.0) - jnp.log(1.0 + jnp.exp(-jnp.abs(G)))
    row = lax.broadcasted_iota(I32, (L, LANES), 0)
    F = lf
    sh = 1
    while sh < L:
        F = F + jnp.where(row >= sh, pltpu.roll(F, sh, axis=0), 0.0)
        sh *= 2
    GT = G.T
    FT = F.T
    ti = lax.broadcasted_iota(I32, (L, L), 0)
    si = lax.broadcasted_iota(I32, (L, L), 1)
    causal = ti >= si
    ones_col = (lax.broadcasted_iota(I32, (L, dp - hd), 1) == 0).astype(BF16)
    ms = ms_ref[...]

    for h in range(nh):
        sl = slice(h * hd, (h + 1) * hd)
        li_c, F_c = G[:, h:h + 1], F[:, nh + h:nh + h + 1]
        li_r, F_r = GT[h:h + 1, :], FT[nh + h:nh + h + 1, :]
        F_last = F_c[L - 1:L, :]
        m_prev = ms[:, h:h + 1]
        a_r = F_last - F_r + li_r
        b = jnp.max(a_r, axis=-1, keepdims=True)
        ea_c = jnp.exp(F_last - F_c + li_c - b)
        logw = jnp.where(causal, F_c - F_r + li_r, NEG)
        m_intra = jnp.max(logw, axis=-1, keepdims=True)
        m_inter = F_c + m_prev
        m_t = jnp.maximum(m_inter, m_intra)
        q = qs[:, sl].astype(BF16)
        k = ks[:, sl]
        v_aug = jnp.concatenate([v_ref[:, sl], ones_col], axis=1)
        s = lax.dot_general(q, k.astype(BF16), (((1,), (1,)), ((), ())),
                            preferred_element_type=F32) * jnp.exp(logw - m_t)
        inter = jnp.exp(m_inter - m_t)
        cn = cn_ref[h]
        num = (jnp.dot(s.astype(BF16), v_aug, preferred_element_type=F32)
               + inter * jnp.dot(q, cn.astype(BF16), preferred_element_type=F32))
        den = num[:, hd:hd + 1]
        hv = num[:, :hd] / jnp.maximum(jnp.abs(den), jnp.exp(-m_t))
        m_new = jnp.maximum(F_last + m_prev, b)
        decay = jnp.exp(F_last + m_prev - m_new)
        inj = jnp.exp(b - m_new)
        kv = jnp.dot((ea_c * k).T.astype(BF16), v_aug, preferred_element_type=F32)
        cn_ref[h] = decay * cn + inj * kv
        ms_ref[:, h:h + 1] = m_new
        hb = _rms(hv, hn_ref[:, sl])
        y_ref[:, sl] = (_sigmoid(ob_ref[:, sl].astype(F32)) * hb).astype(y_ref.dtype)


def mlstm(proj_b, gates, proj_t, conv_w, conv_b3, gbias3, hnorm3, l, cfg):
    S = proj_b.shape[0]
    L, nh, hd = cfg.chunk, cfg.h_b, cfg.hd_b
    wb = nh * hd
    dp = _round_up(hd + 1, LANES)
    assert L == LANES and S % L == 0 and 2 * nh <= LANES and cfg.conv_w <= SUBLANES
    cw = conv_w.shape[1]
    return pl.pallas_call(
        functools.partial(_mlstm_kernel, nh=nh, hd=hd, dp=dp, conv_w=cw),
        grid=(S // L,),
        in_specs=[pl.BlockSpec((L, wb), lambda c: (c, 0)),
                  pl.BlockSpec((L, wb), lambda c: (c, 1)),
                  pl.BlockSpec((L, wb), lambda c: (c, 2)),
                  pl.BlockSpec((L, LANES), lambda c: (c, 0)),
                  pl.BlockSpec((L, wb), lambda c: (c, 0)),
                  pl.BlockSpec((None, cw, 2 * wb), lambda c: (l, 0, 0)),
                  pl.BlockSpec((None, 1, 2 * wb), lambda c: (l, 0, 0)),
                  pl.BlockSpec((None, 1, LANES), lambda c: (l, 0, 0)),
                  pl.BlockSpec((None, 1, wb), lambda c: (l, 0, 0))],
        out_specs=pl.BlockSpec((L, wb), lambda c: (c, 0)),
        out_shape=jax.ShapeDtypeStruct((S, wb), BF16),
        scratch_shapes=[pltpu.VMEM((nh, hd, dp), F32), pltpu.VMEM((1, LANES), F32),
                        pltpu.VMEM((SUBLANES + L, 2 * wb), F32)],
        compiler_params=_cparams(1, 40),
        name="mlstm",
    )(proj_b, proj_b, proj_b, gates, proj_t, conv_w, conv_b3, gbias3, hnorm3)


def _merge_kernel(*refs, lead, tn, gn):
    y_refs, g_refs, w_hbm = refs[0:3], refs[3:6], refs[6:9]
    o_ref = refs[9]
    stages, wbfs, sem = refs[10:13], refs[13:16], refs[16]
    n = pl.program_id(0)
    m = pl.program_id(1)

    def copies(nb):
        col = pl.multiple_of(nb * tn, LANES)
        return [pltpu.make_async_copy(w_hbm[j].at[lead, :, pl.ds(col, tn)], stages[j], sem.at[j])
                for j in range(3)]

    @pl.when((n == 0) & (m == 0))
    def _():
        for c in copies(0):
            c.start()

    @pl.when(m == 0)
    def _():
        for c in copies(n):
            c.wait()
        for j in range(3):
            k_rows = stages[j].shape[0]
            for r in range(0, k_rows, 512):
                sl = slice(r, min(r + 512, k_rows))
                wbfs[j][sl] = stages[j][sl].astype(BF16)

        @pl.when(n + 1 < gn)
        def _():
            for c in copies(n + 1):
                c.start()

    acc = None
    for j in range(3):
        term = _sigmoid(g_refs[j][...].astype(F32)) * jnp.dot(y_refs[j][...], wbfs[j][...],
                                                             preferred_element_type=F32)
        acc = term if acc is None else acc + term
    o_ref[...] = acc.astype(o_ref.dtype)


def gated_merge(ya, yb, ym, w_a, w_b, w_m, gates, gate_col0, l, d, tm=512, tn=1024):
    S = ya.shape[0]
    tm = min(tm, S)
    while gate_col0 % tn or d % tn:
        tn //= 2
    goff = gate_col0 // tn
    nd = d // tn
    ys, wts = (ya, yb, ym), (w_a, w_b, w_m)

    def lhs_spec(y):
        return pl.BlockSpec((tm, y.shape[1]), lambda n, m: (m, 0))

    def g_spec(j):
        return pl.BlockSpec((tm, tn), lambda n, m: (m, goff + j * nd + n))

    ksum = sum(y.shape[1] for y in ys)
    vmem = 3 * tm * ksum * 2 + ksum * tn * 6 + 8 * tm * tn * 2 + 5 * tm * tn * 4
    return pl.pallas_call(
        functools.partial(_merge_kernel, lead=l, tn=tn, gn=nd),
        grid=(nd, S // tm),
        in_specs=[lhs_spec(y) for y in ys] + [g_spec(j) for j in range(3)]
        + [pl.BlockSpec(memory_space=pl.ANY)] * 3,
        out_specs=pl.BlockSpec((tm, tn), lambda n, m: (m, n)),
        out_shape=jax.ShapeDtypeStruct((S, d), BF16),
        scratch_shapes=[pltpu.VMEM((w.shape[1], tn), F32) for w in wts]
        + [pltpu.VMEM((w.shape[1], tn), BF16) for w in wts] + [pltpu.SemaphoreType.DMA((3,))],
        compiler_params=_cparams(2, vmem // MIB + VMEM_SLACK_MIB),
        name="gated_merge",
    )(*ys, gates, gates, gates, *wts)


def _router_kernel(x_ref, g_ref, wr_ref, br_ref, h_ref, meta_ref, gate_ref, cnt_ref, run_ref, *, n_exp):
    i = pl.program_id(0)
    tm = x_ref.shape[0]

    @pl.when(i == 0)
    def _():
        run_ref[...] = jnp.zeros_like(run_ref)

    y = _rms(x_ref[...], g_ref[...])
    h_ref[...] = y
    logits = jnp.dot(y, wr_ref[...], preferred_element_type=F32,
                     precision=lax.Precision.HIGHEST) + br_ref[...]
    lane = lax.broadcasted_iota(I32, (tm, LANES), 1)
    lanef = lane.astype(F32)
    logits = jnp.where(lane < n_exp, logits, -jnp.inf)
    v1 = jnp.max(logits, axis=-1, keepdims=True)
    i1 = jnp.min(jnp.where(logits == v1, lanef, float(LANES)), axis=-1, keepdims=True).astype(I32)
    rest = jnp.where(lane == i1, -jnp.inf, logits)
    v2 = jnp.max(rest, axis=-1, keepdims=True)
    i2 = jnp.min(jnp.where(rest == v2, lanef, float(LANES)), axis=-1, keepdims=True).astype(I32)
    e = jnp.exp(v2 - v1)
    g1 = 1.0 / (1.0 + e)
    g2 = e / (1.0 + e)
    oh1 = lane == i1
    oh2 = lane == i2
    oh = jnp.where(oh1 | oh2, 1.0, 0.0)
    r = lax.broadcasted_iota(I32, (tm, tm), 0)
    cidx = lax.broadcasted_iota(I32, (tm, tm), 1)
    tri = jnp.where(cidx < r, 1.0, 0.0).astype(BF16)
    cum = jnp.dot(tri, oh.astype(BF16), preferred_element_type=F32) + run_ref[...]
    r1 = jnp.sum(jnp.where(oh1, cum, 0.0), axis=-1, keepdims=True).astype(I32)
    r2 = jnp.sum(jnp.where(oh2, cum, 0.0), axis=-1, keepdims=True).astype(I32)
    run_ref[...] += jnp.sum(oh, axis=0, keepdims=True)
    meta_ref[...] = jnp.where(lane == 0, i1, jnp.where(lane == 1, i2,
                              jnp.where(lane == 2, r1, jnp.where(lane == 3, r2, 0))))
    gate_ref[...] = jnp.where(lane == 0, g1, jnp.where(lane == 1, g2, 0.0))
    cnt_ref[...] = run_ref[...]


def norm_router(x, g3, w_router_p, b_router_p, l, lr, n_exp, tm=256):
    S, D = x.shape
    tm = min(tm, S)
    return pl.pallas_call(
        functools.partial(_router_kernel, n_exp=n_exp),
        grid=(S // tm,),
        in_specs=[pl.BlockSpec((tm, D), lambda i: (i, 0)),
                  pl.BlockSpec((None, 1, D), lambda i: (l, 0, 0)),
                  pl.BlockSpec((None, D, LANES), lambda i: (lr, 0, 0)),
                  pl.BlockSpec((None, 1, LANES), lambda i: (lr, 0, 0))],
        out_specs=[pl.BlockSpec((tm, D), lambda i: (i, 0)),
                   pl.BlockSpec((tm, LANES), lambda i: (i, 0)),
                   pl.BlockSpec((tm, LANES), lambda i: (i, 0)),
                   pl.BlockSpec((1, LANES), lambda i: (0, 0))],
        out_shape=[jax.ShapeDtypeStruct((S, D), F32), jax.ShapeDtypeStruct((S, LANES), I32),
                   jax.ShapeDtypeStruct((S, LANES), F32), jax.ShapeDtypeStruct((1, LANES), F32)],
        scratch_shapes=[pltpu.VMEM((1, LANES), F32)],
        compiler_params=_cparams(1, 40),
        name="norm_router",
    )(x, g3, w_router_p, b_router_p)


def _row_copy(src_hbm, dst, src_row, dst_row, sem):
    return pltpu.make_async_copy(src_hbm.at[pl.ds(src_row, 1)], dst.at[pl.ds(dst_row, 1)], sem)


def _gather_kernel(order_ref, nv_ref, h_hbm, xs_ref, buf, sem, *, tg):
    i = pl.program_id(0)
    nv = nv_ref[0]
    slot = i % 2

    def start_tile(t, s):
        def body(it, carry):
            for q in range(2):
                r = 2 * it + q
                _row_copy(h_hbm, buf.at[s], order_ref[t * tg + r], r, sem.at[s]).start(priority=q)
            return carry
        lax.fori_loop(0, tg // 2, body, 0, unroll=4)

    def wait_tile(s):
        def body(r, carry):
            _row_copy(h_hbm, buf.at[s], 0, r, sem.at[s]).wait()
            return carry
        lax.fori_loop(0, tg, body, 0, unroll=8)

    @pl.when(i == 0)
    def _():
        start_tile(0, 0)

    @pl.when(i + 1 < nv)
    def _():
        start_tile(i + 1, 1 - slot)

    @pl.when(i < nv)
    def _():
        wait_tile(slot)
        xs_ref[...] = buf[slot].astype(BF16)

    @pl.when(i >= nv)
    def _():
        xs_ref[...] = jnp.zeros_like(xs_ref)


def gather_rows(h, order, nvalid, n_tiles, tg):
    D = h.shape[1]
    grid_spec = pltpu.PrefetchScalarGridSpec(
        num_scalar_prefetch=2,
        grid=(n_tiles,),
        in_specs=[pl.BlockSpec(memory_space=pl.ANY)],
        out_specs=pl.BlockSpec((tg, D), lambda i, order, nv: (i, 0)),
        scratch_shapes=[pltpu.VMEM((2, tg, D), F32), pltpu.SemaphoreType.DMA((2,))],
    )
    return pl.pallas_call(
        functools.partial(_gather_kernel, tg=tg),
        grid_spec=grid_spec,
        out_shape=jax.ShapeDtypeStruct((n_tiles * tg, D), BF16),
        compiler_params=_cparams(1, 40),
        name="moe_gather",
    )(order, nvalid, h)


def _gmm_kernel(te_ref, src_ref, first_ref, nxt_ref, lastrun_ref, half_ref, nv_ref, *refs, nw, epi, lr, tn):
    x_ref = refs[0]
    w_hbm = refs[1:1 + nw]
    out_ref = refs[1 + nw]
    stage, wbf, sem = refs[2 + nw:5 + nw]
    n = pl.program_id(0)
    i = pl.program_id(1)

    def w_copy(j, e, nb):
        return pltpu.make_async_copy(w_hbm[j].at[lr, e, :, pl.ds(nb * tn, tn)], stage.at[j], sem.at[j])

    @pl.when((n == 0) & (i == 0))
    def _():
        for j in range(nw):
            w_copy(j, te_ref[0], 0).start()

    @pl.when(first_ref[i] == 1)
    def _():
        for j in range(nw):
            w_copy(j, te_ref[i], n).wait()
        _cast_blocks(stage, wbf, nw)
        nb_next = n + lastrun_ref[i]

        @pl.when(nb_next < pl.num_programs(0))
        def _():
            for j in range(nw):
                w_copy(j, nxt_ref[i], nb_next).start()

    def compute(rows):
        x = x_ref[:rows]
        accs = [jnp.dot(x, wbf[j], preferred_element_type=F32) for j in range(nw)]
        if epi == "swiglu":
            g, u = accs
            out = g * _sigmoid(g) * u
        else:
            out = accs[0]
        out_ref[:rows] = out.astype(out_ref.dtype)

    tg = x_ref.shape[0]
    valid = i < nv_ref[0]
    half = half_ref[i] == 1

    @pl.when(valid & jnp.logical_not(half))
    def _():
        compute(tg)

    @pl.when(valid & half)
    def _():
        compute(tg // 2)
        out_ref[tg // 2:] = jnp.zeros((tg - tg // 2, out_ref.shape[1]), out_ref.dtype)

    @pl.when(jnp.logical_not(valid))
    def _():
        out_ref[...] = jnp.zeros_like(out_ref)


def grouped_matmul(xs, ws, lr, tile_meta, *, epi, out_dtype, tg, tn, name):
    P, K = xs.shape
    N = ws[0].shape[-1]
    nw = len(ws)
    tn = min(tn, N)
    assert N % tn == 0 and P % tg == 0
    grid_spec = pltpu.PrefetchScalarGridSpec(
        num_scalar_prefetch=len(tile_meta),
        grid=(N // tn, P // tg),
        in_specs=[pl.BlockSpec((tg, K), lambda n, i, te, src, *_: (src[i], 0))]
        + [pl.BlockSpec(memory_space=pl.ANY)] * nw,
        out_specs=pl.BlockSpec((tg, tn), lambda n, i, *_: (i, n)),
        scratch_shapes=[pltpu.VMEM((nw, K, tn), F32), pltpu.VMEM((nw, K, tn), BF16),
                        pltpu.SemaphoreType.DMA((nw,))],
    )
    obytes = jnp.dtype(out_dtype).itemsize
    vmem = 2 * tg * K * 2 + nw * K * tn * 6 + 2 * tg * tn * obytes + (nw + 1) * tg * tn * 4
    return pl.pallas_call(
        functools.partial(_gmm_kernel, nw=nw, epi=epi, lr=lr, tn=tn),
        grid_spec=grid_spec,
        out_shape=jax.ShapeDtypeStruct((P, N), out_dtype),
        compiler_params=_cparams(2, vmem // MIB + VMEM_SLACK_MIB),
        name=name,
    )(*tile_meta, xs, *ws)


def _combine_kernel(dest_ref, x_ref, gate_ref, ys_hbm, out_ref, buf, sem, *, tc, top_k):
    i = pl.program_id(0)
    slot = i % 2

    def start_tile(t, s):
        def body(r, carry):
            for k in range(top_k):
                _row_copy(ys_hbm, buf.at[s, k], dest_ref[(t * tc + r) * top_k + k], r,
                          sem.at[s]).start(priority=k % 2)
            return carry
        lax.fori_loop(0, tc, body, 0, unroll=4)

    def wait_tile(s):
        def body(r, carry):
            for k in range(top_k):
                _row_copy(ys_hbm, buf.at[s, k], 0, r, sem.at[s]).wait()
            return carry
        lax.fori_loop(0, tc, body, 0, unroll=4)

    @pl.when(i == 0)
    def _():
        start_tile(0, 0)

    @pl.when(i + 1 < pl.num_programs(0))
    def _():
        start_tile(i + 1, 1 - slot)

    wait_tile(slot)
    g = gate_ref[...]
    out = x_ref[...]
    for k in range(top_k):
        out = out + g[:, k:k + 1] * buf[slot, k]
    out_ref[...] = out


def moe_combine(x, gates, ys, dest_flat, top_k, tc=256):
    S, D = x.shape
    tc = min(tc, S)
    grid_spec = pltpu.PrefetchScalarGridSpec(
        num_scalar_prefetch=1,
        grid=(S // tc,),
        in_specs=[pl.BlockSpec((tc, D), lambda i, d: (i, 0)),
                  pl.BlockSpec((tc, LANES), lambda i, d: (i, 0)),
                  pl.BlockSpec(memory_space=pl.ANY)],
        out_specs=pl.BlockSpec((tc, D), lambda i, d: (i, 0)),
        scratch_shapes=[pltpu.VMEM((2, top_k, tc, D), F32), pltpu.SemaphoreType.DMA((2,))],
    )
    return pl.pallas_call(
        functools.partial(_combine_kernel, tc=tc, top_k=top_k),
        grid_spec=grid_spec,
        out_shape=jax.ShapeDtypeStruct((S, D), F32),
        compiler_params=_cparams(1, 48),
        name="moe_combine",
    )(dest_flat, x, gates, ys)


def moe_layer(x, norm_ffn3, l, w_router, b_router, w_e_gate, w_e_up, w_e_down, lr, cfg):
    S, D = x.shape
    E, top_k, tg = cfg.n_experts, 2, cfg.tg
    tg = min(tg, S)
    wr = jnp.pad(w_router, ((0, 0), (0, 0), (0, LANES - E)))
    br = jnp.pad(b_router, ((0, 0), (0, LANES - E)))[:, None, :]
    h, meta, gates, cnt = norm_router(x, norm_ffn3, wr, br, l, lr, E)

    eid, rank = meta[:, 0:top_k], meta[:, top_k:2 * top_k]
    counts = cnt[0, :E].astype(I32)
    padded = (counts + tg - 1) // tg * tg
    ends = jnp.cumsum(padded)
    dest = (ends - padded)[eid] + rank
    n_tiles = (S * top_k) // tg + E
    order = jnp.zeros((n_tiles * tg,), I32).at[dest.reshape(-1)].set(
        jnp.repeat(jnp.arange(S, dtype=I32), top_k))
    nvalid = (ends[-1] // tg).astype(I32)
    tile = jnp.arange(n_tiles, dtype=I32)
    src = jnp.minimum(tile, nvalid - 1)
    te = jnp.sum((src * tg)[:, None] >= ends[None, :], axis=1).astype(I32)
    first = ((tile == 0) | (te != jnp.roll(te, 1))).astype(I32)
    later = (tile[None, :] > tile[:, None]) & (te[None, :] != te[:, None])
    nxt_idx = jnp.min(jnp.where(later, tile[None, :], n_tiles), axis=1)
    lastrun = (nxt_idx == n_tiles).astype(I32)
    nxt = jnp.where(lastrun == 1, te[0], te[jnp.minimum(nxt_idx, n_tiles - 1)]).astype(I32)
    tile_rows = jnp.clip((counts + ends - padded)[te] - src * tg, 0, tg)
    half = (tile_rows <= tg // 2).astype(I32)
    nv = nvalid.reshape(1)
    tile_meta = (te, src, first, nxt, lastrun, half, nv)

    xs = gather_rows(h, order, nv, n_tiles, tg)
    a = grouped_matmul(xs, [w_e_gate, w_e_up], lr, tile_meta, epi="swiglu", out_dtype=BF16,
                       tg=tg, tn=cfg.tn_b, name="moe_gate_up")
    ys = grouped_matmul(a, [w_e_down], lr, tile_meta, epi="plain", out_dtype=F32,
                        tg=tg, tn=cfg.tn_gd, name="moe_down")
    return moe_combine(x, gates, ys, dest.reshape(-1), top_k)


def _forward(x, mem, rel_bias, norm_mix, norm_ffn, norm_mem, w_in, qn_a, kn_a, conv_w, conv_b,
             gate_bias_b, hnorm_b, w_mem_kv, qn_m, kn_m, w_br_a, w_br_b, w_br_m, w_out,
             w_ff_gate, w_ff_up, w_ff_down, w_router, b_router, w_e_gate, w_e_up, w_e_down, cfg):
    B, S, D = x.shape
    assert B == 1 and mem.shape[0] == 1
    depth = norm_mix.shape[0]
    ng = len(cfg.a_groups)
    w_a = ng * cfg.heads_per_group * cfg.hd_a
    w_b = cfg.h_b * cfg.hd_b
    w_m = cfg.h_m * cfg.hd_m
    if_col0 = 3 * w_a + 3 * w_b
    tail_col0 = if_col0 + 2 * cfg.h_b
    d_ff = w_ff_gate.shape[-1]
    tk_down = min(cfg.tk_down, _round_up(d_ff, LANES))
    d_ff_p = _round_up(d_ff, tk_down)

    def row3(p):
        return p[:, None, :]

    x = x.reshape(S, D)
    mem2 = mem.reshape(mem.shape[1], D)
    norm_mix3, norm_ffn3, norm_mem3 = row3(norm_mix), row3(norm_ffn), row3(norm_mem)
    qn_a3, kn_a3, qn_m3, kn_m3 = row3(qn_a), row3(kn_a), row3(qn_m), row3(kn_m)
    conv_b3, hnorm3 = row3(conv_b), row3(hnorm_b)
    gbias3 = row3(jnp.pad(gate_bias_b, ((0, 0), (0, LANES - 2 * cfg.h_b))))
    mm = functools.partial(matmul, tm=cfg.tm)
    gw = cfg.heads_per_group * cfg.hd_a
    w_in_t = jnp.swapaxes(w_in, 1, 2)

    for l in range(depth):
        h = rmsnorm(x, norm_mix3, l)
        proj_g = [mm(h, [w_in_t], lead=l, trans=True, n_out=3 * gw, tn=gw,
                     blk_of=lambda n, g=g: n * ng + g, name=f"proj_a{g}")
                  for g in range(ng)]
        proj_b = mm(h, [w_in_t], lead=l, trans=True, col0=3 * w_a, n_out=3 * w_b, tn=cfg.tn_b,
                    name="proj_b")
        gates_b = mm(h, [w_in_t], lead=l, trans=True, col0=if_col0, n_out=LANES, out_dtype=F32,
                     tn=LANES, name="proj_if")
        proj_t = mm(h, [w_in_t], lead=l, trans=True, col0=tail_col0, n_out=w_b + w_m, tn=cfg.tn_b,
                    name="proj_tail")
        proj_gt = mm(h, [w_in_t], lead=l, trans=True, col0=tail_col0 + w_b + w_m, n_out=3 * D,
                     tn=cfg.tn, name="proj_gates")

        outs, lses = zip(*[band_attention(proj_g[g], rel_bias, qn_a3, kn_a3, l, g, cfg)
                           for g in range(ng)])
        y_a = alpha_merge(outs, lses)
        y_b = mlstm(proj_b, gates_b, proj_t, conv_w, conv_b3, gbias3, hnorm3, l, cfg)
        hm = rmsnorm(mem2, norm_mem3, l)
        kv = mm(hm, [w_mem_kv], lead=l, n_out=2 * w_m, tn=cfg.tn, name="mem_kv")
        y_m = cross_attention(proj_t, w_b, kv, qn_m3, kn_m3, l, cfg)
        y = gated_merge(y_a, y_b, y_m, w_br_a, w_br_b, w_br_m, proj_gt, 0, l, D,
                        tm=cfg.tm // 2, tn=cfg.tn)
        x = mm(y, [w_out], lead=l, n_out=D, epi="residual", res=x, out_dtype=F32, tn=cfg.tn_b,
               name="out_proj")

        if l % 2 == 0:
            ld = l // 2
            h2 = rmsnorm(x, norm_ffn3, l)
            a = matmul(h2, [w_ff_gate, w_ff_up], lead=ld, n_out=d_ff_p, epi="swiglu", n_valid=d_ff,
                       tm=cfg.tm // 2, tn=cfg.tn2, name="ffn_gate_up")
            wd = jnp.pad(w_ff_down[ld], ((0, d_ff_p - d_ff), (0, 0))).astype(BF16)
            x = matmul_ktiled_residual(a, wd, x, tm=cfg.tm, tn=cfg.tn_down, tk=tk_down, name="ffn_down")
        else:
            x = moe_layer(x, norm_ffn3, l, w_router, b_router, w_e_gate, w_e_up, w_e_down, l // 2, cfg)
    return x.reshape(B, S, D)


def kernel(x, mem, rel_bias, norm_mix, norm_ffn, norm_mem, w_in, qn_a, kn_a, conv_w, conv_b, gate_bias_b, hnorm_b, w_mem_kv, qn_m, kn_m, w_br_a, w_br_b, w_br_m, w_out, w_ff_gate, w_ff_up, w_ff_down, w_router, b_router, w_e_gate, w_e_up, w_e_down):
    return _forward(x, mem, rel_bias, norm_mix, norm_ffn, norm_mem, w_in, qn_a, kn_a, conv_w, conv_b,
                    gate_bias_b, hnorm_b, w_mem_kv, qn_m, kn_m, w_br_a, w_br_b, w_br_m, w_out,
                    w_ff_gate, w_ff_up, w_ff_down, w_router, b_router, w_e_gate, w_e_up, w_e_down,
                    Cfg())
```

```python
import functools
from typing import NamedTuple

import numpy as np
import jax
import jax.numpy as jnp
from jax import lax
from jax.experimental import pallas as pl
from jax.experimental.pallas import tpu as pltpu

F32 = jnp.float32
BF16 = jnp.bfloat16
I32 = jnp.int32
EPS = 1e-6
NEG = -1e30
MIB = 1 << 20
LANES = 128
SUBLANES = 8
VMEM_CAP_MIB = 60
VMEM_SLACK_MIB = 12


class Cfg(NamedTuple):
    a_groups: tuple = ((128, 1), (512, 4), (2048, 16))
    heads_per_group: int = 4
    hd_a: int = 128
    band_block: int = 128
    h_b: int = 4
    hd_b: int = 384
    chunk: int = 128
    conv_w: int = 4
    h_m: int = 4
    hd_m: int = 256
    n_buckets: int = 32
    max_dist: int = 2048
    n_experts: int = 8
    tm: int = 1024
    tn: int = 1024
    tn_b: int = 512
    tn2: int = 512
    tn_gd: int = 1024
    tg: int = 512
    tk_down: int = 5504
    tn_down: int = 512


def _cparams(n_axes, vmem_mib):
    return pltpu.CompilerParams(dimension_semantics=("arbitrary",) * n_axes,
                                vmem_limit_bytes=int(min(vmem_mib, VMEM_CAP_MIB)) * MIB)


def _round_up(a, b):
    return -(-a // b) * b


def _sigmoid(x):
    return 1.0 / (1.0 + jnp.exp(-x))


def _rms(x, g):
    return x * lax.rsqrt(jnp.mean(x * x, axis=-1, keepdims=True) + EPS) * g


def _rmsnorm_kernel(x_ref, g_ref, o_ref):
    o_ref[...] = _rms(x_ref[...].astype(F32), g_ref[...]).astype(o_ref.dtype)


def rmsnorm(x, g3, l, out_dtype=BF16, tm=256):
    M, D = x.shape
    tm = min(tm, M)
    return pl.pallas_call(
        _rmsnorm_kernel,
        grid=(M // tm,),
        in_specs=[pl.BlockSpec((tm, D), lambda i: (i, 0)),
                  pl.BlockSpec((None, 1, D), lambda i: (l, 0, 0))],
        out_specs=pl.BlockSpec((tm, D), lambda i: (i, 0)),
        out_shape=jax.ShapeDtypeStruct((M, D), out_dtype),
        compiler_params=_cparams(1, 32),
        name="rmsnorm",
    )(x, g3)


def _cast_blocks(stage, wbf, nw, rows=512):
    n_rows = stage.shape[1]
    for j in range(nw):
        for r in range(0, n_rows, rows):
            sl = slice(r, min(r + rows, n_rows))
            wbf[j, sl] = stage[j, sl].astype(BF16)


def _mm_kernel(lhs_ref, *refs, nw, trans, epi, n_valid, tn, gn, last_w, src_of):
    w_hbm = refs[:nw]
    pos = nw
    res_ref = None
    if epi == "residual":
        res_ref = refs[pos]
        pos += 1
    out_ref = refs[pos]
    stage, wbf, sem = refs[pos + 1:pos + 4]
    n = pl.program_id(0)
    m = pl.program_id(1)

    def copies(nb, width):
        out = []
        for j in range(nw):
            if width == tn:
                dst = stage.at[j]
            elif trans:
                dst = stage.at[j, pl.ds(0, width), :]
            else:
                dst = stage.at[j, :, pl.ds(0, width)]
            out.append(pltpu.make_async_copy(src_of(w_hbm[j], nb, width), dst, sem.at[j]))
        return out

    def for_block(nb, fn):
        if last_w == tn:
            for c in copies(nb, tn):
                fn(c)
        elif isinstance(nb, int):
            for c in copies(nb, last_w if nb == gn - 1 else tn):
                fn(c)
        else:
            @pl.when(nb == gn - 1)
            def _():
                for c in copies(nb, last_w):
                    fn(c)

            @pl.when(nb != gn - 1)
            def _():
                for c in copies(nb, tn):
                    fn(c)

    @pl.when((n == 0) & (m == 0))
    def _():
        for_block(0, lambda c: c.start())

    @pl.when(m == 0)
    def _():
        for_block(n, lambda c: c.wait())
        _cast_blocks(stage, wbf, nw)

        @pl.when(n + 1 < gn)
        def _():
            for_block(n + 1, lambda c: c.start())

    dims = (((1,), (1,)), ((), ())) if trans else (((1,), (0,)), ((), ()))
    accs = [lax.dot_general(lhs_ref[...], wbf[j], dims, preferred_element_type=F32) for j in range(nw)]
    if epi == "plain":
        out = accs[0]
    elif epi == "residual":
        out = res_ref[...] + accs[0]
    else:
        g, u = accs
        out = g * _sigmoid(g) * u
        if n_valid is not None:
            col = pl.program_id(0) * tn + lax.broadcasted_iota(I32, out.shape, 1)
            out = jnp.where(col < n_valid, out, 0.0)
    out_ref[...] = out.astype(out_ref.dtype)


def matmul(lhs, ws, *, lead=None, col0=0, n_out, epi="plain", res=None, n_valid=None,
           trans=False, blk_of=None, out_dtype=BF16, tm=1024, tn=512, name="matmul"):
    M, K = lhs.shape
    nw = len(ws)
    tm = min(tm, M)
    tn = min(tn, _round_up(n_out, LANES))
    assert tn % LANES == 0 and M % tm == 0 and ws[0].dtype == F32
    gn = pl.cdiv(n_out, tn)
    last_w = n_out - (gn - 1) * tn
    if blk_of is None:
        def col_of(nb):
            return col0 + nb * tn
    else:
        def col_of(nb):
            return blk_of(nb) * tn
    if trans:
        n_rows = ws[0].shape[1]
        assert lead is not None and n_rows % SUBLANES == 0 and col0 % SUBLANES == 0 and last_w % SUBLANES == 0
        ws = [w.reshape(-1, K) for w in ws]
        w_block = (tn, K)

        def src_of(w, nb, width):
            return w.at[pl.ds(pl.multiple_of(lead * n_rows + col_of(nb), SUBLANES), width), :]
    else:
        n_cols = ws[0].shape[-1]
        assert col0 % LANES == 0 and last_w % LANES == 0 and n_cols % LANES == 0
        assert col0 + n_out <= n_cols or n_valid is not None
        w_block = (K, tn)

        def src_of(w, nb, width):
            w = w if lead is None else w.at[lead]
            col = jnp.minimum(col_of(nb), n_cols - width)
            return w.at[:, pl.ds(pl.multiple_of(col, LANES), width)]
    in_specs = [pl.BlockSpec((tm, K), lambda n, m: (m, 0))] + [pl.BlockSpec(memory_space=pl.ANY)] * nw
    args = [lhs] + list(ws)
    if epi == "residual":
        in_specs.append(pl.BlockSpec((tm, tn), lambda n, m: (m, n)))
        args.append(res)
    obytes = jnp.dtype(out_dtype).itemsize
    vmem = (3 * tm * K * 2 + nw * K * tn * 6 + 2 * tm * tn * obytes
            + (2 * tm * tn * 4 if epi == "residual" else 0) + (nw + 1) * tm * tn * 4)
    return pl.pallas_call(
        functools.partial(_mm_kernel, nw=nw, trans=trans, epi=epi, n_valid=n_valid, tn=tn, gn=gn,
                          last_w=last_w, src_of=src_of),
        grid=(gn, M // tm),
        in_specs=in_specs,
        out_specs=pl.BlockSpec((tm, tn), lambda n, m: (m, n)),
        out_shape=jax.ShapeDtypeStruct((M, n_out), out_dtype),
        scratch_shapes=[pltpu.VMEM((nw,) + w_block, F32), pltpu.VMEM((nw,) + w_block, BF16),
                        pltpu.SemaphoreType.DMA((nw,))],
        compiler_params=_cparams(2, vmem // MIB + VMEM_SLACK_MIB),
        name=name,
    )(*args)


def _mmk_kernel(lhs_ref, w_ref, res_ref, out_ref, acc_ref):
    k = pl.program_id(2)

    @pl.when(k == 0)
    def _():
        acc_ref[...] = jnp.zeros_like(acc_ref)

    acc_ref[...] += jnp.dot(lhs_ref[...], w_ref[...], preferred_element_type=F32)

    @pl.when(k == pl.num_programs(2) - 1)
    def _():
        out_ref[...] = res_ref[...] + acc_ref[...]


def matmul_ktiled_residual(lhs, w, res, *, tm=1024, tn=1024, tk=1024, name="matmul_k"):
    M, K = lhs.shape
    N = w.shape[1]
    tm, tn, tk = min(tm, M), min(tn, N), min(tk, K)
    assert M % tm == 0 and N % tn == 0 and K % tk == 0
    vmem = 2 * tm * tk * 2 + 2 * tk * tn * 2 + 5 * tm * tn * 4
    return pl.pallas_call(
        _mmk_kernel,
        grid=(M // tm, N // tn, K // tk),
        in_specs=[pl.BlockSpec((tm, tk), lambda m, n, k: (m, k)),
                  pl.BlockSpec((tk, tn), lambda m, n, k: (k, n)),
                  pl.BlockSpec((tm, tn), lambda m, n, k: (m, n))],
        out_specs=pl.BlockSpec((tm, tn), lambda m, n, k: (m, n)),
        out_shape=jax.ShapeDtypeStruct((M, N), F32),
        scratch_shapes=[pltpu.VMEM((tm, tn), F32)],
        compiler_params=_cparams(3, vmem // MIB + VMEM_SLACK_MIB),
        name=name,
    )(lhs, w, res)


def _t5_bucket_np(dist, n_buckets, max_dist):
    max_exact = n_buckets // 2
    d = np.maximum(dist, 1).astype(np.float32)
    large = max_exact + (np.log(d / np.float32(max_exact)) / np.float32(np.log(max_dist / max_exact))
                         * np.float32(n_buckets - max_exact)).astype(np.int32)
    large = np.minimum(large, n_buckets - 1)
    return np.where(dist < max_exact, dist, large).astype(np.int32)


def _band_kernel(tab_ref, bkt_ref, q_ref, kp_ref, kc_ref, vp_ref, vc_ref, qn_ref, kn_ref,
                 o_ref, lse_ref, bias_ref, *, hpg, hd, head0, buckets, scale):
    p = pl.program_id(0)
    n = pl.program_id(1)
    bb = kp_ref.shape[0]
    qb = q_ref.shape[0] // bb

    @pl.when((p == 0) & (n == 0))
    def _():
        bkt = bkt_ref[...]
        for h in range(hpg):
            acc = jnp.full((bb, 2 * bb), NEG, F32)
            for b in buckets:
                acc = jnp.where(bkt == b, tab_ref[b, head0 + h], acc)
            bias_ref[h] = acc

    ki = lax.broadcasted_iota(I32, (bb, 2 * bb), 1)
    kvalid = (ki >= bb) | (n > 0)
    for h in range(hpg):
        sl = slice(h * hd, (h + 1) * hd)
        q_all = (_rms(q_ref[:, sl].astype(F32), qn_ref[...]) * scale).astype(BF16)
        k_all = jnp.concatenate([kp_ref[:, sl], kc_ref[:, sl]], axis=0).astype(F32)
        k_all = _rms(k_all, kn_ref[...]).astype(BF16)
        v_all = jnp.concatenate([vp_ref[:, sl], vc_ref[:, sl]], axis=0)
        for j in range(qb):
            rows = slice(j * bb, (j + 1) * bb)
            keys = slice(j * bb, (j + 2) * bb)
            s = lax.dot_general(q_all[rows], k_all[keys], (((1,), (1,)), ((), ())),
                                preferred_element_type=F32) + bias_ref[h]
            if j == 0:
                s = jnp.where(kvalid, s, NEG)
            m = jnp.max(s, axis=-1, keepdims=True)
            e = jnp.exp(s - m)
            l = jnp.sum(e, axis=-1, keepdims=True)
            o = jnp.dot((e / l).astype(BF16), v_all[keys], preferred_element_type=F32)
            o_ref[rows, sl] = o.astype(o_ref.dtype)
            lse_ref[rows, sl] = jnp.broadcast_to(m + jnp.log(l), (bb, hd))


def band_attention(proj_g, rel_bias, qn3, kn3, l, g, cfg):
    S, npa = proj_g.shape
    win, dil = cfg.a_groups[g]
    steps = win // dil
    bb, hpg, hd = cfg.band_block, cfg.heads_per_group, cfg.hd_a
    gw = hpg * hd
    ls = S // dil
    assert S % dil == 0 and ls % bb == 0 and steps <= bb and npa == 3 * gw
    nblk = ls // bb
    nb = 3
    x = proj_g.reshape(ls, dil * npa)

    qi = np.arange(bb)[:, None]
    ki = np.arange(2 * bb)[None, :]
    rel = qi + bb - ki
    inside = (rel >= 0) & (rel <= steps)
    bkt = np.where(inside, _t5_bucket_np(np.maximum(rel, 0) * dil, cfg.n_buckets, cfg.max_dist), -1)
    buckets = tuple(int(b) for b in np.unique(bkt[inside]))

    qb = max(d for d in (4, 2, 1) if nblk % d == 0)

    def qmap(p, n):
        return (n, p * nb)

    def kmap_c(p, n):
        return (n, p * nb + 1)

    def kmap_p(p, n):
        return (jnp.maximum(n * qb - 1, 0), p * nb + 1)

    def vmap_c(p, n):
        return (n, p * nb + 2)

    def vmap_p(p, n):
        return (jnp.maximum(n * qb - 1, 0), p * nb + 2)

    blk = (qb * bb, gw)
    blk_p = (bb, gw)
    o, lse = pl.pallas_call(
        functools.partial(_band_kernel, hpg=hpg, hd=hd, head0=g * hpg, buckets=buckets,
                          scale=float(hd) ** -0.5),
        grid=(dil, nblk // qb),
        in_specs=[pl.BlockSpec(memory_space=pltpu.SMEM),
                  pl.BlockSpec((bb, 2 * bb), lambda p, n: (0, 0)),
                  pl.BlockSpec(blk, qmap), pl.BlockSpec(blk_p, kmap_p), pl.BlockSpec(blk, kmap_c),
                  pl.BlockSpec(blk_p, vmap_p), pl.BlockSpec(blk, vmap_c),
                  pl.BlockSpec((None, 1, hd), lambda p, n: (l, 0, 0)),
                  pl.BlockSpec((None, 1, hd), lambda p, n: (l, 0, 0))],
        out_specs=[pl.BlockSpec(blk, lambda p, n: (n, p)), pl.BlockSpec(blk, lambda p, n: (n, p))],
        out_shape=[jax.ShapeDtypeStruct((ls, dil * gw), BF16),
                   jax.ShapeDtypeStruct((ls, dil * gw), F32)],
        scratch_shapes=[pltpu.VMEM((hpg, bb, 2 * bb), F32)],
        compiler_params=_cparams(2, 32),
        name=f"band_attn_g{g}",
    )(rel_bias, jnp.asarray(bkt, I32), x, x, x, x, x, qn3, kn3)
    return o.reshape(S, gw), lse.reshape(S, gw)


def _alpha_kernel(*refs, ng, gw):
    o_refs, l_refs, y_ref = refs[:ng], refs[ng:2 * ng], refs[2 * ng]
    ls = [r[...] for r in l_refs]
    m = functools.reduce(jnp.maximum, ls)
    es = [jnp.exp(v - m) for v in ls]
    den = functools.reduce(lambda a, b: a + b, es)
    for g in range(ng):
        y_ref[:, g * gw:(g + 1) * gw] = (es[g] / den * o_refs[g][...].astype(F32)).astype(y_ref.dtype)


def alpha_merge(outs, lses, tm=512):
    ng = len(outs)
    S, gw = outs[0].shape
    tm = min(tm, S)
    spec = pl.BlockSpec((tm, gw), lambda i: (i, 0))
    return pl.pallas_call(
        functools.partial(_alpha_kernel, ng=ng, gw=gw),
        grid=(S // tm,),
        in_specs=[spec] * (2 * ng),
        out_specs=pl.BlockSpec((tm, ng * gw), lambda i: (i, 0)),
        out_shape=jax.ShapeDtypeStruct((S, ng * gw), BF16),
        compiler_params=_cparams(1, 32),
        name="alpha_merge",
    )(*outs, *lses)


def _cross_kernel(q_ref, k_ref, v_ref, qn_ref, kn_ref, o_ref, *, scale):
    q = _rms(q_ref[...].astype(F32), qn_ref[...]) * scale
    k = _rms(k_ref[...].astype(F32), kn_ref[...])
    s = lax.dot_general(q.astype(BF16), k.astype(BF16), (((1,), (1,)), ((), ())),
                        preferred_element_type=F32)
    m = jnp.max(s, axis=-1, keepdims=True)
    e = jnp.exp(s - m)
    pr = e / jnp.sum(e, axis=-1, keepdims=True)
    o_ref[...] = jnp.dot(pr.astype(BF16), v_ref[...], preferred_element_type=F32).astype(o_ref.dtype)


def cross_attention(proj_t, q_col0, kv, qn3, kn3, l, cfg, tm=1024):
    S = proj_t.shape[0]
    mlen = kv.shape[0]
    hm, hd = cfg.h_m, cfg.hd_m
    tm = min(tm, S)
    assert q_col0 % hd == 0
    qoff = q_col0 // hd
    return pl.pallas_call(
        functools.partial(_cross_kernel, scale=float(hd) ** -0.5),
        grid=(S // tm, hm),
        in_specs=[pl.BlockSpec((tm, hd), lambda i, h: (i, qoff + h)),
                  pl.BlockSpec((mlen, hd), lambda i, h: (0, h)),
                  pl.BlockSpec((mlen, hd), lambda i, h: (0, hm + h)),
                  pl.BlockSpec((None, 1, hd), lambda i, h: (l, 0, 0)),
                  pl.BlockSpec((None, 1, hd), lambda i, h: (l, 0, 0))],
        out_specs=pl.BlockSpec((tm, hd), lambda i, h: (i, h)),
        out_shape=jax.ShapeDtypeStruct((S, hm * hd), BF16),
        compiler_params=_cparams(2, 32),
        name="cross_attn",
    )(proj_t, kv, kv, qn3, kn3)


def _mlstm_kernel(q_ref, k_ref, v_ref, g_ref, ob_ref, cw_ref, cb_ref, gb_ref, hn_ref,
                  y_ref, cn_ref, ms_ref, xw_ref, *, nh, hd, dp, conv_w):
    c = pl.program_id(0)
    L = q_ref.shape[0]
    wb = nh * hd

    @pl.when(c == 0)
    def _():
        cn_ref[...] = jnp.zeros_like(cn_ref)
        ms_ref[...] = jnp.zeros_like(ms_ref)
        xw_ref[:SUBLANES] = jnp.zeros((SUBLANES, 2 * wb), F32)

    def conv_silu(x_ref, col0):
        cols = slice(col0, col0 + wb)
        x = x_ref[...].astype(F32)
        xw_ref[SUBLANES:, cols] = x
        y = cb_ref[:, cols] + cw_ref[conv_w - 1:conv_w, cols] * x
        for s in range(1, conv_w):
            y = y + cw_ref[conv_w - 1 - s:conv_w - s, cols] * xw_ref[SUBLANES - s:SUBLANES - s + L, cols]
        xw_ref[:SUBLANES, cols] = x[L - SUBLANES:]
        return y * _sigmoid(y)

    qs = conv_silu(q_ref, 0)
    ks = conv_silu(k_ref, wb) * (float(hd) ** -0.5)

    G = g_ref[...] + gb_ref[...]
    lf = jnp.minimum(G, 0.0) - jnp.log(1.0 + jnp.exp(-jnp.abs(G)))
    row = lax.broadcasted_iota(I32, (L, LANES), 0)
    F = lf
    sh = 1
    while sh < L:
        F = F + jnp.where(row >= sh, pltpu.roll(F, sh, axis=0), 0.0)
        sh *= 2
    GT = G.T
    FT = F.T
    ti = lax.broadcasted_iota(I32, (L, L), 0)
    si = lax.broadcasted_iota(I32, (L, L), 1)
    causal = ti >= si
    ones_col = (lax.broadcasted_iota(I32, (L, dp - hd), 1) == 0).astype(BF16)
    ms = ms_ref[...]

    for h in range(nh):
        sl = slice(h * hd, (h + 1) * hd)
        li_c, F_c = G[:, h:h + 1], F[:, nh + h:nh + h + 1]
        li_r, F_r = GT[h:h + 1, :], FT[nh + h:nh + h + 1, :]
        F_last = F_c[L - 1:L, :]
        m_prev = ms[:, h:h + 1]
        a_r = F_last - F_r + li_r
        b = jnp.max(a_r, axis=-1, keepdims=True)
        ea_c = jnp.exp(F_last - F_c + li_c - b)
        logw = jnp.where(causal, F_c - F_r + li_r, NEG)
        m_intra = jnp.max(logw, axis=-1, keepdims=True)
        m_inter = F_c + m_prev
        m_t = jnp.maximum(m_inter, m_intra)
        q = qs[:, sl].astype(BF16)
        k = ks[:, sl]
        v_aug = jnp.concatenate([v_ref[:, sl], ones_col], axis=1)
        s = lax.dot_general(q, k.astype(BF16), (((1,), (1,)), ((), ())),
                            preferred_element_type=F32) * jnp.exp(logw - m_t)
        inter = jnp.exp(m_inter - m_t)
        cn = cn_ref[h]
        num = (jnp.dot(s.astype(BF16), v_aug, preferred_element_type=F32)
               + inter * jnp.dot(q, cn.astype(BF16), preferred_element_type=F32))
        den = num[:, hd:hd + 1]
        hv = num[:, :hd] / jnp.maximum(jnp.abs(den), jnp.exp(-m_t))
        m_new = jnp.maximum(F_last + m_prev, b)
        decay = jnp.exp(F_last + m_prev - m_new)
        inj = jnp.exp(b - m_new)
        kv = jnp.dot((ea_c * k).T.astype(BF16), v_aug, preferred_element_type=F32)
        cn_ref[h] = decay * cn + inj * kv
        ms_ref[:, h:h + 1] = m_new
        hb = _rms(hv, hn_ref[:, sl])
        y_ref[:, sl] = (_sigmoid(ob_ref[:, sl].astype(F32)) * hb).astype(y_ref.dtype)


def mlstm(proj_b, gates, proj_t, conv_w, conv_b3, gbias3, hnorm3, l, cfg):
    S = proj_b.shape[0]
    L, nh, hd = cfg.chunk, cfg.h_b, cfg.hd_b
    wb = nh * hd
    dp = _round_up(hd + 1, LANES)
    assert L == LANES and S % L == 0 and 2 * nh <= LANES and cfg.conv_w <= SUBLANES
    cw = conv_w.shape[1]
    return pl.pallas_call(
        functools.partial(_mlstm_kernel, nh=nh, hd=hd, dp=dp, conv_w=cw),
        grid=(S // L,),
        in_specs=[pl.BlockSpec((L, wb), lambda c: (c, 0)),
                  pl.BlockSpec((L, wb), lambda c: (c, 1)),
                  pl.BlockSpec((L, wb), lambda c: (c, 2)),
                  pl.BlockSpec((L, LANES), lambda c: (c, 0)),
                  pl.BlockSpec((L, wb), lambda c: (c, 0)),
                  pl.BlockSpec((None, cw, 2 * wb), lambda c: (l, 0, 0)),
                  pl.BlockSpec((None, 1, 2 * wb), lambda c: (l, 0, 0)),
                  pl.BlockSpec((None, 1, LANES), lambda c: (l, 0, 0)),
                  pl.BlockSpec((None, 1, wb), lambda c: (l, 0, 0))],
        out_specs=pl.BlockSpec((L, wb), lambda c: (c, 0)),
        out_shape=jax.ShapeDtypeStruct((S, wb), BF16),
        scratch_shapes=[pltpu.VMEM((nh, hd, dp), F32), pltpu.VMEM((1, LANES), F32),
                        pltpu.VMEM((SUBLANES + L, 2 * wb), F32)],
        compiler_params=_cparams(1, 40),
        name="mlstm",
    )(proj_b, proj_b, proj_b, gates, proj_t, conv_w, conv_b3, gbias3, hnorm3)


def _merge_kernel(*refs, lead, tn, gn):
    y_refs, g_refs, w_hbm = refs[0:3], refs[3:6], refs[6:9]
    o_ref = refs[9]
    stages, wbfs, sem = refs[10:13], refs[13:16], refs[16]
    n = pl.program_id(0)
    m = pl.program_id(1)

    def copies(nb):
        col = pl.multiple_of(nb * tn, LANES)
        return [pltpu.make_async_copy(w_hbm[j].at[lead, :, pl.ds(col, tn)], stages[j], sem.at[j])
                for j in range(3)]

    @pl.when((n == 0) & (m == 0))
    def _():
        for c in copies(0):
            c.start()

    @pl.when(m == 0)
    def _():
        for c in copies(n):
            c.wait()
        for j in range(3):
            k_rows = stages[j].shape[0]
            for r in range(0, k_rows, 512):
                sl = slice(r, min(r + 512, k_rows))
                wbfs[j][sl] = stages[j][sl].astype(BF16)

        @pl.when(n + 1 < gn)
        def _():
            for c in copies(n + 1):
                c.start()

    acc = None
    for j in range(3):
        term = _sigmoid(g_refs[j][...].astype(F32)) * jnp.dot(y_refs[j][...], wbfs[j][...],
                                                             preferred_element_type=F32)
        acc = term if acc is None else acc + term
    o_ref[...] = acc.astype(o_ref.dtype)


def gated_merge(ya, yb, ym, w_a, w_b, w_m, gates, gate_col0, l, d, tm=512, tn=1024):
    S = ya.shape[0]
    tm = min(tm, S)
    while gate_col0 % tn or d % tn:
        tn //= 2
    goff = gate_col0 // tn
    nd = d // tn
    ys, wts = (ya, yb, ym), (w_a, w_b, w_m)

    def lhs_spec(y):
        return pl.BlockSpec((tm, y.shape[1]), lambda n, m: (m, 0))

    def g_spec(j):
        return pl.BlockSpec((tm, tn), lambda n, m: (m, goff + j * nd + n))

    ksum = sum(y.shape[1] for y in ys)
    vmem = 3 * tm * ksum * 2 + ksum * tn * 6 + 8 * tm * tn * 2 + 5 * tm * tn * 4
    return pl.pallas_call(
        functools.partial(_merge_kernel, lead=l, tn=tn, gn=nd),
        grid=(nd, S // tm),
        in_specs=[lhs_spec(y) for y in ys] + [g_spec(j) for j in range(3)]
        + [pl.BlockSpec(memory_space=pl.ANY)] * 3,
        out_specs=pl.BlockSpec((tm, tn), lambda n, m: (m, n)),
        out_shape=jax.ShapeDtypeStruct((S, d), BF16),
        scratch_shapes=[pltpu.VMEM((w.shape[1], tn), F32) for w in wts]
        + [pltpu.VMEM((w.shape[1], tn), BF16) for w in wts] + [pltpu.SemaphoreType.DMA((3,))],
        compiler_params=_cparams(2, vmem // MIB + VMEM_SLACK_MIB),
        name="gated_merge",
    )(*ys, gates, gates, gates, *wts)


def _router_kernel(x_ref, g_ref, wr_ref, br_ref, h_ref, meta_ref, gate_ref, cnt_ref, run_ref, *, n_exp):
    i = pl.program_id(0)
    tm = x_ref.shape[0]

    @pl.when(i == 0)
    def _():
        run_ref[...] = jnp.zeros_like(run_ref)

    y = _rms(x_ref[...], g_ref[...])
    h_ref[...] = y
    logits = jnp.dot(y, wr_ref[...], preferred_element_type=F32,
                     precision=lax.Precision.HIGHEST) + br_ref[...]
    lane = lax.broadcasted_iota(I32, (tm, LANES), 1)
    lanef = lane.astype(F32)
    logits = jnp.where(lane < n_exp, logits, -jnp.inf)
    v1 = jnp.max(logits, axis=-1, keepdims=True)
    i1 = jnp.min(jnp.where(logits == v1, lanef, float(LANES)), axis=-1, keepdims=True).astype(I32)
    rest = jnp.where(lane == i1, -jnp.inf, logits)
    v2 = jnp.max(rest, axis=-1, keepdims=True)
    i2 = jnp.min(jnp.where(rest == v2, lanef, float(LANES)), axis=-1, keepdims=True).astype(I32)
    e = jnp.exp(v2 - v1)
    g1 = 1.0 / (1.0 + e)
    g2 = e / (1.0 + e)
    oh1 = lane == i1
    oh2 = lane == i2
    oh = jnp.where(oh1 | oh2, 1.0, 0.0)
    r = lax.broadcasted_iota(I32, (tm, tm), 0)
    cidx = lax.broadcasted_iota(I32, (tm, tm), 1)
    tri = jnp.where(cidx < r, 1.0, 0.0).astype(BF16)
    cum = jnp.dot(tri, oh.astype(BF16), preferred_element_type=F32) + run_ref[...]
    r1 = jnp.sum(jnp.where(oh1, cum, 0.0), axis=-1, keepdims=True).astype(I32)
    r2 = jnp.sum(jnp.where(oh2, cum, 0.0), axis=-1, keepdims=True).astype(I32)
    run_ref[...] += jnp.sum(oh, axis=0, keepdims=True)
    meta_ref[...] = jnp.where(lane == 0, i1, jnp.where(lane == 1, i2,
                              jnp.where(lane == 2, r1, jnp.where(lane == 3, r2, 0))))
    gate_ref[...] = jnp.where(lane == 0, g1, jnp.where(lane == 1, g2, 0.0))
    cnt_ref[...] = run_ref[...]


def norm_router(x, g3, w_router_p, b_router_p, l, lr, n_exp, tm=256):
    S, D = x.shape
    tm = min(tm, S)
    return pl.pallas_call(
        functools.partial(_router_kernel, n_exp=n_exp),
        grid=(S // tm,),
        in_specs=[pl.BlockSpec((tm, D), lambda i: (i, 0)),
                  pl.BlockSpec((None, 1, D), lambda i: (l, 0, 0)),
                  pl.BlockSpec((None, D, LANES), lambda i: (lr, 0, 0)),
                  pl.BlockSpec((None, 1, LANES), lambda i: (lr, 0, 0))],
        out_specs=[pl.BlockSpec((tm, D), lambda i: (i, 0)),
                   pl.BlockSpec((tm, LANES), lambda i: (i, 0)),
                   pl.BlockSpec((tm, LANES), lambda i: (i, 0)),
                   pl.BlockSpec((1, LANES), lambda i: (0, 0))],
        out_shape=[jax.ShapeDtypeStruct((S, D), F32), jax.ShapeDtypeStruct((S, LANES), I32),
                   jax.ShapeDtypeStruct((S, LANES), F32), jax.ShapeDtypeStruct((1, LANES), F32)],
        scratch_shapes=[pltpu.VMEM((1, LANES), F32)],
        compiler_params=_cparams(1, 40),
        name="norm_router",
    )(x, g3, w_router_p, b_router_p)


def _row_copy(src_hbm, dst, src_row, dst_row, sem):
    return pltpu.make_async_copy(src_hbm.at[pl.ds(src_row, 1)], dst.at[pl.ds(dst_row, 1)], sem)


def _gather_kernel(order_ref, nv_ref, h_hbm, xs_ref, buf, sem, *, tg):
    i = pl.program_id(0)
    nv = nv_ref[0]
    slot = i % 2

    def start_tile(t, s):
        def body(it, carry):
            for q in range(2):
                r = 2 * it + q
                _row_copy(h_hbm, buf.at[s], order_ref[t * tg + r], r, sem.at[s]).start(priority=q)
            return carry
        lax.fori_loop(0, tg // 2, body, 0, unroll=4)

    def wait_tile(s):
        def body(r, carry):
            _row_copy(h_hbm, buf.at[s], 0, r, sem.at[s]).wait()
            return carry
        lax.fori_loop(0, tg, body, 0, unroll=8)

    @pl.when(i == 0)
    def _():
        start_tile(0, 0)

    @pl.when(i + 1 < nv)
    def _():
        start_tile(i + 1, 1 - slot)

    @pl.when(i < nv)
    def _():
        wait_tile(slot)
        xs_ref[...] = buf[slot].astype(BF16)

    @pl.when(i >= nv)
    def _():
        xs_ref[...] = jnp.zeros_like(xs_ref)


def gather_rows(h, order, nvalid, n_tiles, tg):
    D = h.shape[1]
    grid_spec = pltpu.PrefetchScalarGridSpec(
        num_scalar_prefetch=2,
        grid=(n_tiles,),
        in_specs=[pl.BlockSpec(memory_space=pl.ANY)],
        out_specs=pl.BlockSpec((tg, D), lambda i, order, nv: (i, 0)),
        scratch_shapes=[pltpu.VMEM((2, tg, D), F32), pltpu.SemaphoreType.DMA((2,))],
    )
    return pl.pallas_call(
        functools.partial(_gather_kernel, tg=tg),
        grid_spec=grid_spec,
        out_shape=jax.ShapeDtypeStruct((n_tiles * tg, D), BF16),
        compiler_params=_cparams(1, 40),
        name="moe_gather",
    )(order, nvalid, h)


def _gmm_kernel(te_ref, src_ref, first_ref, nxt_ref, lastrun_ref, half_ref, nv_ref, *refs, nw, epi, lr, tn):
    x_ref = refs[0]
    w_hbm = refs[1:1 + nw]
    out_ref = refs[1 + nw]
    stage, wbf, sem = refs[2 + nw:5 + nw]
    n = pl.program_id(0)
    i = pl.program_id(1)

    def w_copy(j, e, nb):
        return pltpu.make_async_copy(w_hbm[j].at[lr, e, :, pl.ds(nb * tn, tn)], stage.at[j], sem.at[j])

    @pl.when((n == 0) & (i == 0))
    def _():
        for j in range(nw):
            w_copy(j, te_ref[0], 0).start()

    @pl.when(first_ref[i] == 1)
    def _():
        for j in range(nw):
            w_copy(j, te_ref[i], n).wait()
        _cast_blocks(stage, wbf, nw)
        nb_next = n + lastrun_ref[i]

        @pl.when(nb_next < pl.num_programs(0))
        def _():
            for j in range(nw):
                w_copy(j, nxt_ref[i], nb_next).start()

    def compute(rows):
        x = x_ref[:rows]
        accs = [jnp.dot(x, wbf[j], preferred_element_type=F32) for j in range(nw)]
        if epi == "swiglu":
            g, u = accs
            out = g * _sigmoid(g) * u
        else:
            out = accs[0]
        out_ref[:rows] = out.astype(out_ref.dtype)

    tg = x_ref.shape[0]
    valid = i < nv_ref[0]
    half = half_ref[i] == 1

    @pl.when(valid & jnp.logical_not(half))
    def _():
        compute(tg)

    @pl.when(valid & half)
    def _():
        compute(tg // 2)
        out_ref[tg // 2:] = jnp.zeros((tg - tg // 2, out_ref.shape[1]), out_ref.dtype)

    @pl.when(jnp.logical_not(valid))
    def _():
        out_ref[...] = jnp.zeros_like(out_ref)


def grouped_matmul(xs, ws, lr, tile_meta, *, epi, out_dtype, tg, tn, name):
    P, K = xs.shape
    N = ws[0].shape[-1]
    nw = len(ws)
    tn = min(tn, N)
    assert N % tn == 0 and P % tg == 0
    grid_spec = pltpu.PrefetchScalarGridSpec(
        num_scalar_prefetch=len(tile_meta),
        grid=(N // tn, P // tg),
        in_specs=[pl.BlockSpec((tg, K), lambda n, i, te, src, *_: (src[i], 0))]
        + [pl.BlockSpec(memory_space=pl.ANY)] * nw,
        out_specs=pl.BlockSpec((tg, tn), lambda n, i, *_: (i, n)),
        scratch_shapes=[pltpu.VMEM((nw, K, tn), F32), pltpu.VMEM((nw, K, tn), BF16),
                        pltpu.SemaphoreType.DMA((nw,))],
    )
    obytes = jnp.dtype(out_dtype).itemsize
    vmem = 2 * tg * K * 2 + nw * K * tn * 6 + 2 * tg * tn * obytes + (nw + 1) * tg * tn * 4
    return pl.pallas_call(
        functools.partial(_gmm_kernel, nw=nw, epi=epi, lr=lr, tn=tn),
        grid_spec=grid_spec,
        out_shape=jax.ShapeDtypeStruct((P, N), out_dtype),
        compiler_params=_cparams(2, vmem // MIB + VMEM_SLACK_MIB),
        name=name,
    )(*tile_meta, xs, *ws)


def _combine_kernel(dest_ref, x_ref, gate_ref, ys_hbm, out_ref, buf, sem, *, tc, top_k):
    i = pl.program_id(0)
    slot = i % 2

    def start_tile(t, s):
        def body(r, carry):
            for k in range(top_k):
                _row_copy(ys_hbm, buf.at[s, k], dest_ref[(t * tc + r) * top_k + k], r,
                          sem.at[s]).start(priority=k % 2)
            return carry
        lax.fori_loop(0, tc, body, 0, unroll=4)

    def wait_tile(s):
        def body(r, carry):
            for k in range(top_k):
                _row_copy(ys_hbm, buf.at[s, k], 0, r, sem.at[s]).wait()
            return carry
        lax.fori_loop(0, tc, body, 0, unroll=4)

    @pl.when(i == 0)
    def _():
        start_tile(0, 0)

    @pl.when(i + 1 < pl.num_programs(0))
    def _():
        start_tile(i + 1, 1 - slot)

    wait_tile(slot)
    g = gate_ref[...]
    out = x_ref[...]
    for k in range(top_k):
        out = out + g[:, k:k + 1] * buf[slot, k]
    out_ref[...] = out


def moe_combine(x, gates, ys, dest_flat, top_k, tc=256):
    S, D = x.shape
    tc = min(tc, S)
    grid_spec = pltpu.PrefetchScalarGridSpec(
        num_scalar_prefetch=1,
        grid=(S // tc,),
        in_specs=[pl.BlockSpec((tc, D), lambda i, d: (i, 0)),
                  pl.BlockSpec((tc, LANES), lambda i, d: (i, 0)),
                  pl.BlockSpec(memory_space=pl.ANY)],
        out_specs=pl.BlockSpec((tc, D), lambda i, d: (i, 0)),
        scratch_shapes=[pltpu.VMEM((2, top_k, tc, D), F32), pltpu.SemaphoreType.DMA((2,))],
    )
    return pl.pallas_call(
        functools.partial(_combine_kernel, tc=tc, top_k=top_k),
        grid_spec=grid_spec,
        out_shape=jax.ShapeDtypeStruct((S, D), F32),
        compiler_params=_cparams(1, 48),
        name="moe_combine",
    )(dest_flat, x, gates, ys)


def moe_layer(x, norm_ffn3, l, w_router, b_router, w_e_gate, w_e_up, w_e_down, lr, cfg):
    S, D = x.shape
    E, top_k, tg = cfg.n_experts, 2, cfg.tg
    tg = min(tg, S)
    wr = jnp.pad(w_router, ((0, 0), (0, 0), (0, LANES - E)))
    br = jnp.pad(b_router, ((0, 0), (0, LANES - E)))[:, None, :]
    h, meta, gates, cnt = norm_router(x, norm_ffn3, wr, br, l, lr, E)

    eid, rank = meta[:, 0:top_k], meta[:, top_k:2 * top_k]
    counts = cnt[0, :E].astype(I32)
    padded = (counts + tg - 1) // tg * tg
    ends = jnp.cumsum(padded)
    dest = (ends - padded)[eid] + rank
    n_tiles = (S * top_k) // tg + E
    order = jnp.zeros((n_tiles * tg,), I32).at[dest.reshape(-1)].set(
        jnp.repeat(jnp.arange(S, dtype=I32), top_k))
    nvalid = (ends[-1] // tg).astype(I32)
    tile = jnp.arange(n_tiles, dtype=I32)
    src = jnp.minimum(tile, nvalid - 1)
    te = jnp.sum((src * tg)[:, None] >= ends[None, :], axis=1).astype(I32)
    first = ((tile == 0) | (te != jnp.roll(te, 1))).astype(I32)
    later = (tile[None, :] > tile[:, None]) & (te[None, :] != te[:, None])
    nxt_idx = jnp.min(jnp.where(later, tile[None, :], n_tiles), axis=1)
    lastrun = (nxt_idx == n_tiles).astype(I32)
    nxt = jnp.where(lastrun == 1, te[0], te[jnp.minimum(nxt_idx, n_tiles - 1)]).astype(I32)
    tile_rows = jnp.clip((counts + ends - padded)[te] - src * tg, 0, tg)
    half = (tile_rows <= tg // 2).astype(I32)
    nv = nvalid.reshape(1)
    tile_meta = (te, src, first, nxt, lastrun, half, nv)

    xs = gather_rows(h, order, nv, n_tiles, tg)
    a = grouped_matmul(xs, [w_e_gate, w_e_up], lr, tile_meta, epi="swiglu", out_dtype=BF16,
                       tg=tg, tn=cfg.tn_b, name="moe_gate_up")
    ys = grouped_matmul(a, [w_e_down], lr, tile_meta, epi="plain", out_dtype=F32,
                        tg=tg, tn=cfg.tn_gd, name="moe_down")
    return moe_combine(x, gates, ys, dest.reshape(-1), top_k)


def _forward(x, mem, rel_bias, norm_mix, norm_ffn, norm_mem, w_in, qn_a, kn_a, conv_w, conv_b,
             gate_bias_b, hnorm_b, w_mem_kv, qn_m, kn_m, w_br_a, w_br_b, w_br_m, w_out,
             w_ff_gate, w_ff_up, w_ff_down, w_router, b_router, w_e_gate, w_e_up, w_e_down, cfg):
    B, S, D = x.shape
    assert B == 1 and mem.shape[0] == 1
    depth = norm_mix.shape[0]
    ng = len(cfg.a_groups)
    w_a = ng * cfg.heads_per_group * cfg.hd_a
    w_b = cfg.h_b * cfg.hd_b
    w_m = cfg.h_m * cfg.hd_m
    if_col0 = 3 * w_a + 3 * w_b
    tail_col0 = if_col0 + 2 * cfg.h_b
    d_ff = w_ff_gate.shape[-1]
    tk_down = min(cfg.tk_down, _round_up(d_ff, LANES))
    d_ff_p = _round_up(d_ff, tk_down)

    def row3(p):
        return p[:, None, :]

    x = x.reshape(S, D)
    mem2 = mem.reshape(mem.shape[1], D)
    norm_mix3, norm_ffn3, norm_mem3 = row3(norm_mix), row3(norm_ffn), row3(norm_mem)
    qn_a3, kn_a3, qn_m3, kn_m3 = row3(qn_a), row3(kn_a), row3(qn_m), row3(kn_m)
    conv_b3, hnorm3 = row3(conv_b), row3(hnorm_b)
    gbias3 = row3(jnp.pad(gate_bias_b, ((0, 0), (0, LANES - 2 * cfg.h_b))))
    mm = functools.partial(matmul, tm=cfg.tm)
    gw = cfg.heads_per_group * cfg.hd_a
    w_in_t = jnp.swapaxes(w_in, 1, 2)

    for l in range(depth):
        h = rmsnorm(x, norm_mix3, l)
        proj_g = [mm(h, [w_in_t], lead=l, trans=True, n_out=3 * gw, tn=gw,
                     blk_of=lambda n, g=g: n * ng + g, name=f"proj_a{g}")
                  for g in range(ng)]
        proj_b = mm(h, [w_in_t], lead=l, trans=True, col0=3 * w_a, n_out=3 * w_b, tn=cfg.tn_b,
                    name="proj_b")
        gates_b = mm(h, [w_in_t], lead=l, trans=True, col0=if_col0, n_out=LANES, out_dtype=F32,
                     tn=LANES, name="proj_if")
        proj_t = mm(h, [w_in_t], lead=l, trans=True, col0=tail_col0, n_out=w_b + w_m, tn=cfg.tn_b,
                    name="proj_tail")
        proj_gt = mm(h, [w_in_t], lead=l, trans=True, col0=tail_col0 + w_b + w_m, n_out=3 * D,
                     tn=cfg.tn, name="proj_gates")

        outs, lses = zip(*[band_attention(proj_g[g], rel_bias, qn_a3, kn_a3, l, g, cfg)
                           for g in range(ng)])
        y_a = alpha_merge(outs, lses)
        y_b = mlstm(proj_b, gates_b, proj_t, conv_w, conv_b3, gbias3, hnorm3, l, cfg)
        hm = rmsnorm(mem2, norm_mem3, l)
        kv = mm(hm, [w_mem_kv], lead=l, n_out=2 * w_m, tn=cfg.tn, name="mem_kv")
        y_m = cross_attention(proj_t, w_b, kv, qn_m3, kn_m3, l, cfg)
        y = gated_merge(y_a, y_b, y_m, w_br_a, w_br_b, w_br_m, proj_gt, 0, l, D,
                        tm=cfg.tm // 2, tn=cfg.tn)
        x = mm(y, [w_out], lead=l, n_out=D, epi="residual", res=x, out_dtype=F32, tn=cfg.tn_b,
               name="out_proj")

        if l % 2 == 0:
            ld = l // 2
            h2 = rmsnorm(x, norm_ffn3, l)
            a = matmul(h2, [w_ff_gate, w_ff_up], lead=ld, n_out=d_ff_p, epi="swiglu", n_valid=d_ff,
                       tm=cfg.tm // 2, tn=cfg.tn2, name="ffn_gate_up")
            wd = jnp.pad(w_ff_down[ld], ((0, d_ff_p - d_ff), (0, 0))).astype(BF16)
            x = matmul_ktiled_residual(a, wd, x, tm=cfg.tm, tn=cfg.tn_down, tk=tk_down, name="ffn_down")
        else:
            x = moe_layer(x, norm_ffn3, l, w_router, b_router, w_e_gate, w_e_up, w_e_down, l // 2, cfg)
    return x.reshape(B, S, D)


def kernel(x, mem, rel_bias, norm_mix, norm_ffn, norm_mem, w_in, qn_a, kn_a, conv_w, conv_b, gate_bias_b, hnorm_b, w_mem_kv, qn_m, kn_m, w_br_a, w_br_b, w_br_m, w_out, w_ff_gate, w_ff_up, w_ff_down, w_router, b_router, w_e_gate, w_e_up, w_e_down):
    return _forward(x, mem, rel_bias, norm_mix, norm_ffn, norm_mem, w_in, qn_a, kn_a, conv_w, conv_b,
                    gate_bias_b, hnorm_b, w_mem_kv, qn_m, kn_m, w_br_a, w_br_b, w_br_m, w_out,
                    w_ff_gate, w_ff_up, w_ff_down, w_router, b_router, w_e_gate, w_e_up, w_e_down,
                    Cfg())
```

```python
import functools
from typing import NamedTuple

import numpy as np
import jax
import jax.numpy as jnp
from jax import lax
from jax.experimental import pallas as pl
from jax.experimental.pallas import tpu as pltpu

F32 = jnp.float32
BF16 = jnp.bfloat16
I32 = jnp.int32
EPS = 1e-6
NEG = -1e30
MIB = 1 << 20
LANES = 128
SUBLANES = 8
VMEM_CAP_MIB = 60
VMEM_SLACK_MIB = 12


class Cfg(NamedTuple):
    a_groups: tuple = ((128, 1), (512, 4), (2048, 16))
    heads_per_group: int = 4
    hd_a: int = 128
    band_block: int = 128
    h_b: int = 4
    hd_b: int = 384
    chunk: int = 128
    conv_w: int = 4
    h_m: int = 4
    hd_m: int = 256
    n_buckets: int = 32
    max_dist: int = 2048
    n_experts: int = 8
    tm: int = 1024
    tn: int = 1024
    tn_b: int = 512
    tn2: int = 512
    tn_gd: int = 1024
    tg: int = 512
    tk_down: int = 5504
    tn_down: int = 512


def _cparams(n_axes, vmem_mib):
    return pltpu.CompilerParams(dimension_semantics=("arbitrary",) * n_axes,
                                vmem_limit_bytes=int(min(vmem_mib, VMEM_CAP_MIB)) * MIB)


def _round_up(a, b):
    return -(-a // b) * b


def _sigmoid(x):
    return 1.0 / (1.0 + jnp.exp(-x))


def _rms(x, g):
    return x * lax.rsqrt(jnp.mean(x * x, axis=-1, keepdims=True) + EPS) * g


def _rmsnorm_kernel(x_ref, g_ref, o_ref):
    o_ref[...] = _rms(x_ref[...].astype(F32), g_ref[...]).astype(o_ref.dtype)


def rmsnorm(x, g3, l, out_dtype=BF16, tm=256):
    M, D = x.shape
    tm = min(tm, M)
    return pl.pallas_call(
        _rmsnorm_kernel,
        grid=(M // tm,),
        in_specs=[pl.BlockSpec((tm, D), lambda i: (i, 0)),
                  pl.BlockSpec((None, 1, D), lambda i: (l, 0, 0))],
        out_specs=pl.BlockSpec((tm, D), lambda i: (i, 0)),
        out_shape=jax.ShapeDtypeStruct((M, D), out_dtype),
        compiler_params=_cparams(1, 32),
        name="rmsnorm",
    )(x, g3)


def _cast_blocks(stage, wbf, nw, rows=512):
    n_rows = stage.shape[1]
    for j in range(nw):
        for r in range(0, n_rows, rows):
            sl = slice(r, min(r + rows, n_rows))
            wbf[j, sl] = stage[j, sl].astype(BF16)


def _mm_kernel(lhs_ref, *refs, nw, trans, epi, n_valid, tn, gn, last_w, src_of, phases):
    w_hbm = refs[:nw]
    pos = nw
    res_ref = None
    if epi == "residual":
        res_ref = refs[pos]
        pos += 1
    out_ref = refs[pos]
    stage, wbf, sem = refs[pos + 1:pos + 4]
    n = pl.program_id(0)
    m = pl.program_id(1)

    def copies(nb, width):
        out = []
        for j in range(nw):
            if width == tn:
                dst = stage.at[j]
            elif trans:
                dst = stage.at[j, pl.ds(0, width), :]
            else:
                dst = stage.at[j, :, pl.ds(0, width)]
            out.append(pltpu.make_async_copy(src_of(w_hbm[j], nb, width), dst, sem.at[j]))
        return out

    def for_block(nb, fn):
        if last_w == tn:
            for c in copies(nb, tn):
                fn(c)
        elif isinstance(nb, int):
            for c in copies(nb, last_w if nb == gn - 1 else tn):
                fn(c)
        else:
            @pl.when(nb == gn - 1)
            def _():
                for c in copies(nb, last_w):
                    fn(c)

            @pl.when(nb != gn - 1)
            def _():
                for c in copies(nb, tn):
                    fn(c)

    @pl.when((n == 0) & (m == 0))
    def _():
        for_block(0, lambda c: c.start())

    @pl.when(m == 0)
    def _():
        for_block(n, lambda c: c.wait())
        _cast_blocks(stage, wbf, nw)

        @pl.when(n + 1 < gn)
        def _():
            for_block(n + 1, lambda c: c.start())

    dims = (((1,), (1,)), ((), ())) if trans else (((1,), (0,)), ((), ()))
    accs = [lax.dot_general(lhs_ref[...], wbf[j], dims, preferred_element_type=F32) for j in range(nw)]
    if epi == "plain":
        out = accs[0]
    elif epi == "residual":
        out = res_ref[...] + accs[0]
    else:
        g, u = accs
        out = g * _sigmoid(g) * u
        if n_valid is not None:
            col = pl.program_id(0) * tn + lax.broadcasted_iota(I32, out.shape, 1)
            out = jnp.where(col < n_valid, out, 0.0)
    if phases == 1:
        out_ref[...] = out.astype(out_ref.dtype)
    else:
        pm_ref = refs[pos + 4]
        tm = out.shape[0]
        for s in range(tn // LANES):
            pm_ref[s] = out[:, s * LANES:(s + 1) * LANES]
        for p in range(phases):
            for s in range(tn // LANES):
                out_ref[p, :, s * LANES:(s + 1) * LANES] = pm_ref[
                    s, pl.ds(p, tm // phases, stride=phases), :].astype(out_ref.dtype)


def matmul(lhs, ws, *, lead=None, col0=0, n_out, epi="plain", res=None, n_valid=None, phases=1,
           trans=False, blk_of=None, out_dtype=BF16, tm=1024, tn=512, name="matmul"):
    M, K = lhs.shape
    nw = len(ws)
    tm = min(tm, M)
    tn = min(tn, _round_up(n_out, LANES))
    assert tn % LANES == 0 and M % tm == 0 and ws[0].dtype == F32
    gn = pl.cdiv(n_out, tn)
    last_w = n_out - (gn - 1) * tn
    if blk_of is None:
        def col_of(nb):
            return col0 + nb * tn
    else:
        def col_of(nb):
            return blk_of(nb) * tn
    if trans:
        n_rows = ws[0].shape[1]
        assert lead is not None and n_rows % SUBLANES == 0 and col0 % SUBLANES == 0 and last_w % SUBLANES == 0
        ws = [w.reshape(-1, K) for w in ws]
        w_block = (tn, K)

        def src_of(w, nb, width):
            return w.at[pl.ds(pl.multiple_of(lead * n_rows + col_of(nb), SUBLANES), width), :]
    else:
        n_cols = ws[0].shape[-1]
        assert col0 % LANES == 0 and last_w % LANES == 0 and n_cols % LANES == 0
        assert col0 + n_out <= n_cols or n_valid is not None
        w_block = (K, tn)

        def src_of(w, nb, width):
            w = w if lead is None else w.at[lead]
            col = jnp.minimum(col_of(nb), n_cols - width)
            return w.at[:, pl.ds(pl.multiple_of(col, LANES), width)]
    in_specs = [pl.BlockSpec((tm, K), lambda n, m: (m, 0))] + [pl.BlockSpec(memory_space=pl.ANY)] * nw
    args = [lhs] + list(ws)
    if epi == "residual":
        in_specs.append(pl.BlockSpec((tm, tn), lambda n, m: (m, n)))
        args.append(res)
    obytes = jnp.dtype(out_dtype).itemsize
    vmem = (3 * tm * K * 2 + nw * K * tn * 6 + 2 * tm * tn * obytes
            + (2 * tm * tn * 4 if epi == "residual" else 0) + (nw + 1 + (phases > 1)) * tm * tn * 4)
    scratch = [pltpu.VMEM((nw,) + w_block, F32), pltpu.VMEM((nw,) + w_block, BF16),
               pltpu.SemaphoreType.DMA((nw,))]
    if phases == 1:
        out_spec = pl.BlockSpec((tm, tn), lambda n, m: (m, n))
        out_shape = jax.ShapeDtypeStruct((M, n_out), out_dtype)
    else:
        assert tm % (phases * 2 * SUBLANES) == 0 and last_w == tn and epi == "plain"
        out_spec = pl.BlockSpec((phases, tm // phases, tn), lambda n, m: (0, m, n))
        out_shape = jax.ShapeDtypeStruct((phases, M // phases, n_out), out_dtype)
        scratch.append(pltpu.VMEM((tn // LANES, tm, LANES), F32))
    return pl.pallas_call(
        functools.partial(_mm_kernel, nw=nw, trans=trans, epi=epi, n_valid=n_valid, tn=tn, gn=gn,
                          last_w=last_w, src_of=src_of, phases=phases),
        grid=(gn, M // tm),
        in_specs=in_specs,
        out_specs=out_spec,
        out_shape=out_shape,
        scratch_shapes=scratch,
        compiler_params=_cparams(2, vmem // MIB + VMEM_SLACK_MIB),
        name=name,
    )(*args)


def _mmk_kernel(lhs_ref, w_ref, res_ref, out_ref, acc_ref):
    k = pl.program_id(2)

    @pl.when(k == 0)
    def _():
        acc_ref[...] = jnp.zeros_like(acc_ref)

    acc_ref[...] += jnp.dot(lhs_ref[...], w_ref[...], preferred_element_type=F32)

    @pl.when(k == pl.num_programs(2) - 1)
    def _():
        out_ref[...] = res_ref[...] + acc_ref[...]


def matmul_ktiled_residual(lhs, w, res, *, tm=1024, tn=1024, tk=1024, name="matmul_k"):
    M, K = lhs.shape
    N = w.shape[1]
    tm, tn, tk = min(tm, M), min(tn, N), min(tk, K)
    assert M % tm == 0 and N % tn == 0 and K % tk == 0
    vmem = 2 * tm * tk * 2 + 2 * tk * tn * 2 + 5 * tm * tn * 4
    return pl.pallas_call(
        _mmk_kernel,
        grid=(M // tm, N // tn, K // tk),
        in_specs=[pl.BlockSpec((tm, tk), lambda m, n, k: (m, k)),
                  pl.BlockSpec((tk, tn), lambda m, n, k: (k, n)),
                  pl.BlockSpec((tm, tn), lambda m, n, k: (m, n))],
        out_specs=pl.BlockSpec((tm, tn), lambda m, n, k: (m, n)),
        out_shape=jax.ShapeDtypeStruct((M, N), F32),
        scratch_shapes=[pltpu.VMEM((tm, tn), F32)],
        compiler_params=_cparams(3, vmem // MIB + VMEM_SLACK_MIB),
        name=name,
    )(lhs, w, res)


def _t5_bucket_np(dist, n_buckets, max_dist):
    max_exact = n_buckets // 2
    d = np.maximum(dist, 1).astype(np.float32)
    large = max_exact + (np.log(d / np.float32(max_exact)) / np.float32(np.log(max_dist / max_exact))
                         * np.float32(n_buckets - max_exact)).astype(np.int32)
    large = np.minimum(large, n_buckets - 1)
    return np.where(dist < max_exact, dist, large).astype(np.int32)


def _band_kernel(tab_ref, bkt_ref, q_ref, kp_ref, kc_ref, vp_ref, vc_ref, qn_ref, kn_ref,
                 o_ref, lse_ref, bias_ref, *, hpg, hd, head0, buckets, scale):
    p = pl.program_id(0)
    n = pl.program_id(1)
    bb = kp_ref.shape[0]
    qb = q_ref.shape[0] // bb

    @pl.when((p == 0) & (n == 0))
    def _():
        bkt = bkt_ref[...]
        for h in range(hpg):
            acc = jnp.full((bb, 2 * bb), NEG, F32)
            for b in buckets:
                acc = jnp.where(bkt == b, tab_ref[b, head0 + h], acc)
            bias_ref[h] = acc

    ki = lax.broadcasted_iota(I32, (bb, 2 * bb), 1)
    kvalid = (ki >= bb) | (n > 0)
    for h in range(hpg):
        sl = slice(h * hd, (h + 1) * hd)
        q_all = (_rms(q_ref[:, sl].astype(F32), qn_ref[...]) * scale).astype(BF16)
        k_all = jnp.concatenate([kp_ref[:, sl], kc_ref[:, sl]], axis=0).astype(F32)
        k_all = _rms(k_all, kn_ref[...]).astype(BF16)
        v_all = jnp.concatenate([vp_ref[:, sl], vc_ref[:, sl]], axis=0)
        for j in range(qb):
            rows = slice(j * bb, (j + 1) * bb)
            keys = slice(j * bb, (j + 2) * bb)
            s = lax.dot_general(q_all[rows], k_all[keys], (((1,), (1,)), ((), ())),
                                preferred_element_type=F32) + bias_ref[h]
            if j == 0:
                s = jnp.where(kvalid, s, NEG)
            m = jnp.max(s, axis=-1, keepdims=True)
            e = jnp.exp(s - m)
            l = jnp.sum(e, axis=-1, keepdims=True)
            o = jnp.dot((e / l).astype(BF16), v_all[keys], preferred_element_type=F32)
            o_ref[rows, sl] = o.astype(o_ref.dtype)
            lse_ref[rows, sl] = jnp.broadcast_to(m + jnp.log(l), (bb, hd))


def band_attention(x, rel_bias, qn3, kn3, l, g, cfg):
    win, dil = cfg.a_groups[g]
    assert x.shape[0] == dil
    _, ls, npa = x.shape
    steps = win // dil
    bb, hpg, hd = cfg.band_block, cfg.heads_per_group, cfg.hd_a
    gw = hpg * hd
    assert ls % bb == 0 and steps <= bb and npa == 3 * gw
    nblk = ls // bb

    qi = np.arange(bb)[:, None]
    ki = np.arange(2 * bb)[None, :]
    rel = qi + bb - ki
    inside = (rel >= 0) & (rel <= steps)
    bkt = np.where(inside, _t5_bucket_np(np.maximum(rel, 0) * dil, cfg.n_buckets, cfg.max_dist), -1)
    buckets = tuple(int(b) for b in np.unique(bkt[inside]))

    qb = max(d for d in (4, 2, 1) if nblk % d == 0)

    def cur(col):
        return pl.BlockSpec((None, qb * bb, gw), lambda p, n: (p, n, col))

    def prev(col):
        return pl.BlockSpec((None, bb, gw), lambda p, n: (p, jnp.maximum(n * qb - 1, 0), col))

    return pl.pallas_call(
        functools.partial(_band_kernel, hpg=hpg, hd=hd, head0=g * hpg, buckets=buckets,
                          scale=float(hd) ** -0.5),
        grid=(dil, nblk // qb),
        in_specs=[pl.BlockSpec(memory_space=pltpu.SMEM),
                  pl.BlockSpec((bb, 2 * bb), lambda p, n: (0, 0)),
                  cur(0), prev(1), cur(1), prev(2), cur(2),
                  pl.BlockSpec((None, 1, hd), lambda p, n: (l, 0, 0)),
                  pl.BlockSpec((None, 1, hd), lambda p, n: (l, 0, 0))],
        out_specs=[cur(0), cur(0)],
        out_shape=[jax.ShapeDtypeStruct((dil, ls, gw), BF16),
                   jax.ShapeDtypeStruct((dil, ls, gw), F32)],
        scratch_shapes=[pltpu.VMEM((hpg, bb, 2 * bb), F32)],
        compiler_params=_cparams(2, 32),
        name=f"band_attn_g{g}",
    )(rel_bias, jnp.asarray(bkt, I32), x, x, x, x, x, qn3, kn3)


def _alpha_kernel(*refs, dils, gw):
    ng = len(dils)
    o_refs, l_refs, y_ref = refs[:ng], refs[ng:2 * ng], refs[2 * ng]
    scr = list(refs[2 * ng + 1:])
    tm = y_ref.shape[0]

    def token_order(ref, dil):
        if dil == 1:
            return ref[0].astype(F32)
        buf = scr.pop(0)
        for p in range(dil):
            for s in range(gw // LANES):
                buf[s, pl.ds(p, tm // dil, stride=dil), :] = ref[p, :, s * LANES:(s + 1) * LANES].astype(F32)
        return jnp.concatenate([buf[s] for s in range(gw // LANES)], axis=1)

    ls = [token_order(r, d) for r, d in zip(l_refs, dils)]
    m = functools.reduce(jnp.maximum, ls)
    es = [jnp.exp(v - m) for v in ls]
    den = functools.reduce(lambda a, b: a + b, es)
    for g in range(ng):
        o = token_order(o_refs[g], dils[g])
        y_ref[:, g * gw:(g + 1) * gw] = (es[g] / den * o).astype(y_ref.dtype)


def alpha_merge(outs, lses, tm=512):
    ng = len(outs)
    dils = tuple(o.shape[0] for o in outs)
    gw = outs[0].shape[2]
    S = outs[0].shape[0] * outs[0].shape[1]
    tm = min(tm, S)
    specs = [pl.BlockSpec((d, tm // d, gw), lambda i: (0, i, 0)) for d in dils]
    n_scr = 2 * sum(d > 1 for d in dils)
    return pl.pallas_call(
        functools.partial(_alpha_kernel, dils=dils, gw=gw),
        grid=(S // tm,),
        in_specs=specs + specs,
        out_specs=pl.BlockSpec((tm, ng * gw), lambda i: (i, 0)),
        out_shape=jax.ShapeDtypeStruct((S, ng * gw), BF16),
        scratch_shapes=[pltpu.VMEM((gw // LANES, tm, LANES), F32)] * n_scr,
        compiler_params=_cparams(1, 40),
        name="alpha_merge",
    )(*outs, *lses)


def _cross_kernel(q_ref, k_ref, v_ref, qn_ref, kn_ref, o_ref, *, scale):
    q = _rms(q_ref[...].astype(F32), qn_ref[...]) * scale
    k = _rms(k_ref[...].astype(F32), kn_ref[...])
    s = lax.dot_general(q.astype(BF16), k.astype(BF16), (((1,), (1,)), ((), ())),
                        preferred_element_type=F32)
    m = jnp.max(s, axis=-1, keepdims=True)
    e = jnp.exp(s - m)
    pr = e / jnp.sum(e, axis=-1, keepdims=True)
    o_ref[...] = jnp.dot(pr.astype(BF16), v_ref[...], preferred_element_type=F32).astype(o_ref.dtype)


def cross_attention(proj_t, q_col0, kv, qn3, kn3, l, cfg, tm=1024):
    S = proj_t.shape[0]
    mlen = kv.shape[0]
    hm, hd = cfg.h_m, cfg.hd_m
    tm = min(tm, S)
    assert q_col0 % hd == 0
    qoff = q_col0 // hd
    return pl.pallas_call(
        functools.partial(_cross_kernel, scale=float(hd) ** -0.5),
        grid=(S // tm, hm),
        in_specs=[pl.BlockSpec((tm, hd), lambda i, h: (i, qoff + h)),
                  pl.BlockSpec((mlen, hd), lambda i, h: (0, h)),
                  pl.BlockSpec((mlen, hd), lambda i, h: (0, hm + h)),
                  pl.BlockSpec((None, 1, hd), lambda i, h: (l, 0, 0)),
                  pl.BlockSpec((None, 1, hd), lambda i, h: (l, 0, 0))],
        out_specs=pl.BlockSpec((tm, hd), lambda i, h: (i, h)),
        out_shape=jax.ShapeDtypeStruct((S, hm * hd), BF16),
        compiler_params=_cparams(2, 32),
        name="cross_attn",
    )(proj_t, kv, kv, qn3, kn3)


def _mlstm_kernel(q_ref, k_ref, v_ref, g_ref, ob_ref, cw_ref, cb_ref, gb_ref, hn_ref,
                  y_ref, cn_ref, ms_ref, xw_ref, *, nh, hd, dp, conv_w):
    c = pl.program_id(0)
    L = q_ref.shape[0]
    wb = nh * hd

    @pl.when(c == 0)
    def _():
        cn_ref[...] = jnp.zeros_like(cn_ref)
        ms_ref[...] = jnp.zeros_like(ms_ref)
        xw_ref[...] = jnp.zeros_like(xw_ref)

    r = lax.broadcasted_iota(I32, ((conv_w - 1) * L, 2 * L), 0)
    cidx = lax.broadcasted_iota(I32, ((conv_w - 1) * L, 2 * L), 1)
    shift_mat = jnp.where(cidx == L + r % L - (r // L + 1), 1.0, 0.0).astype(BF16)

    def conv_silu(x_ref, col0):
        cols = slice(col0, col0 + wb)
        xb = x_ref[...]
        shifted = jnp.dot(shift_mat, jnp.concatenate([xw_ref[:, cols], xb], axis=0),
                          preferred_element_type=F32)
        y = cb_ref[:, cols] + cw_ref[conv_w - 1:conv_w, cols] * xb.astype(F32)
        for s in range(1, conv_w):
            y = y + cw_ref[conv_w - 1 - s:conv_w - s, cols] * shifted[(s - 1) * L:s * L]
        xw_ref[:, cols] = xb
        return y * _sigmoid(y)

    qs = conv_silu(q_ref, 0)
    ks = conv_silu(k_ref, wb) * (float(hd) ** -0.5)

    G = g_ref[...] + gb_ref[...]
    lf = jnp.minimum(G, 0.0) - jnp.log(1.0 + jnp.exp(-jnp.abs(G)))
    row = lax.broadcasted_iota(I32, (L, LANES), 0)
    F = lf
    sh = 1
    while sh < L:
        F = F + jnp.where(row >= sh, pltpu.roll(F, sh, axis=0), 0.0)
        sh *= 2
    GT = G.T
    FT = F.T
    ti = lax.broadcasted_iota(I32, (L, L), 0)
    si = lax.broadcasted_iota(I32, (L, L), 1)
    causal = ti >= si
    ones_col = (lax.broadcasted_iota(I32, (L, dp - hd), 1) == 0).astype(BF16)
    ms = ms_ref[...]

    for h in range(nh):
        sl = slice(h * hd, (h + 1) * hd)
        li_c, F_c = G[:, h:h + 1], F[:, nh + h:nh + h + 1]
        li_r, F_r = GT[h:h + 1, :], FT[nh + h:nh + h + 1, :]
        F_last = F_c[L - 1:L, :]
        m_prev = ms[:, h:h + 1]
        a_r = F_last - F_r + li_r
        b = jnp.max(a_r, axis=-1, keepdims=True)
        ea_c = jnp.exp(F_last - F_c + li_c - b)
        logw = jnp.where(causal, F_c - F_r + li_r, NEG)
        m_intra = jnp.max(logw, axis=-1, keepdims=True)
        m_inter = F_c + m_prev
        m_t = jnp.maximum(m_inter, m_intra)
        q = qs[:, sl].astype(BF16)
        k = ks[:, sl]
        v_aug = jnp.concatenate([v_ref[:, sl], ones_col], axis=1)
        s = lax.dot_general(q, k.astype(BF16), (((1,), (1,)), ((), ())),
                            preferred_element_type=F32) * jnp.exp(logw - m_t)
        inter = jnp.exp(m_inter - m_t)
        cn = cn_ref[h]
        num = (jnp.dot(s.astype(BF16), v_aug, preferred_element_type=F32)
               + inter * jnp.dot(q, cn.astype(BF16), preferred_element_type=F32))
        den = num[:, hd:hd + 1]
        hv = num[:, :hd] / jnp.maximum(jnp.abs(den), jnp.exp(-m_t))
        m_new = jnp.maximum(F_last + m_prev, b)
        decay = jnp.exp(F_last + m_prev - m_new)
        inj = jnp.exp(b - m_new)
        kv = jnp.dot(((inj * ea_c) * k).T.astype(BF16), v_aug, preferred_element_type=F32)
        cn_ref[h] = decay * cn + kv
        ms_ref[:, h:h + 1] = m_new
        hb = _rms(hv, hn_ref[:, sl])
        y_ref[:, sl] = (_sigmoid(ob_ref[:, sl].astype(F32)) * hb).astype(y_ref.dtype)


def mlstm(proj_b, gates, proj_t, conv_w, conv_b3, gbias3, hnorm3, l, cfg):
    S = proj_b.shape[0]
    L, nh, hd = cfg.chunk, cfg.h_b, cfg.hd_b
    wb = nh * hd
    dp = _round_up(hd + 1, LANES)
    assert L == LANES and S % L == 0 and 2 * nh <= LANES and cfg.conv_w <= SUBLANES
    cw = conv_w.shape[1]
    return pl.pallas_call(
        functools.partial(_mlstm_kernel, nh=nh, hd=hd, dp=dp, conv_w=cw),
        grid=(S // L,),
        in_specs=[pl.BlockSpec((L, wb), lambda c: (c, 0)),
                  pl.BlockSpec((L, wb), lambda c: (c, 1)),
                  pl.BlockSpec((L, wb), lambda c: (c, 2)),
                  pl.BlockSpec((L, LANES), lambda c: (c, 0)),
                  pl.BlockSpec((L, wb), lambda c: (c, 0)),
                  pl.BlockSpec((None, cw, 2 * wb), lambda c: (l, 0, 0)),
                  pl.BlockSpec((None, 1, 2 * wb), lambda c: (l, 0, 0)),
                  pl.BlockSpec((None, 1, LANES), lambda c: (l, 0, 0)),
                  pl.BlockSpec((None, 1, wb), lambda c: (l, 0, 0))],
        out_specs=pl.BlockSpec((L, wb), lambda c: (c, 0)),
        out_shape=jax.ShapeDtypeStruct((S, wb), BF16),
        scratch_shapes=[pltpu.VMEM((nh, hd, dp), F32), pltpu.VMEM((1, LANES), F32),
                        pltpu.VMEM((L, 2 * wb), BF16)],
        compiler_params=_cparams(1, 40),
        name="mlstm",
    )(proj_b, proj_b, proj_b, gates, proj_t, conv_w, conv_b3, gbias3, hnorm3)


def _merge_kernel(*refs, lead, tn, gn):
    y_refs, g_refs, w_hbm = refs[0:3], refs[3:6], refs[6:9]
    o_ref = refs[9]
    stages, wbfs, sem = refs[10:13], refs[13:16], refs[16]
    n = pl.program_id(0)
    m = pl.program_id(1)

    def copies(nb):
        col = pl.multiple_of(nb * tn, LANES)
        return [pltpu.make_async_copy(w_hbm[j].at[lead, :, pl.ds(col, tn)], stages[j], sem.at[j])
                for j in range(3)]

    @pl.when((n == 0) & (m == 0))
    def _():
        for c in copies(0):
            c.start()

    @pl.when(m == 0)
    def _():
        for c in copies(n):
            c.wait()
        for j in range(3):
            k_rows = stages[j].shape[0]
            for r in range(0, k_rows, 512):
                sl = slice(r, min(r + 512, k_rows))
                wbfs[j][sl] = stages[j][sl].astype(BF16)

        @pl.when(n + 1 < gn)
        def _():
            for c in copies(n + 1):
                c.start()

    acc = None
    for j in range(3):
        term = _sigmoid(g_refs[j][...].astype(F32)) * jnp.dot(y_refs[j][...], wbfs[j][...],
                                                             preferred_element_type=F32)
        acc = term if acc is None else acc + term
    o_ref[...] = acc.astype(o_ref.dtype)


def gated_merge(ya, yb, ym, w_a, w_b, w_m, gates, gate_col0, l, d, tm=512, tn=1024):
    S = ya.shape[0]
    tm = min(tm, S)
    while gate_col0 % tn or d % tn:
        tn //= 2
    goff = gate_col0 // tn
    nd = d // tn
    ys, wts = (ya, yb, ym), (w_a, w_b, w_m)

    def lhs_spec(y):
        return pl.BlockSpec((tm, y.shape[1]), lambda n, m: (m, 0))

    def g_spec(j):
        return pl.BlockSpec((tm, tn), lambda n, m: (m, goff + j * nd + n))

    ksum = sum(y.shape[1] for y in ys)
    vmem = 3 * tm * ksum * 2 + ksum * tn * 6 + 8 * tm * tn * 2 + 5 * tm * tn * 4
    return pl.pallas_call(
        functools.partial(_merge_kernel, lead=l, tn=tn, gn=nd),
        grid=(nd, S // tm),
        in_specs=[lhs_spec(y) for y in ys] + [g_spec(j) for j in range(3)]
        + [pl.BlockSpec(memory_space=pl.ANY)] * 3,
        out_specs=pl.BlockSpec((tm, tn), lambda n, m: (m, n)),
        out_shape=jax.ShapeDtypeStruct((S, d), BF16),
        scratch_shapes=[pltpu.VMEM((w.shape[1], tn), F32) for w in wts]
        + [pltpu.VMEM((w.shape[1], tn), BF16) for w in wts] + [pltpu.SemaphoreType.DMA((3,))],
        compiler_params=_cparams(2, vmem // MIB + VMEM_SLACK_MIB),
        name="gated_merge",
    )(*ys, gates, gates, gates, *wts)


def _router_kernel(x_ref, g_ref, wr_ref, br_ref, h_ref, meta_ref, gate_ref, cnt_ref, run_ref, *, n_exp):
    i = pl.program_id(0)
    tm = x_ref.shape[0]

    @pl.when(i == 0)
    def _():
        run_ref[...] = jnp.zeros_like(run_ref)

    y = _rms(x_ref[...], g_ref[...])
    h_ref[...] = y
    logits = jnp.dot(y, wr_ref[...], preferred_element_type=F32,
                     precision=lax.Precision.HIGHEST) + br_ref[...]
    lane = lax.broadcasted_iota(I32, (tm, LANES), 1)
    lanef = lane.astype(F32)
    logits = jnp.where(lane < n_exp, logits, -jnp.inf)
    v1 = jnp.max(logits, axis=-1, keepdims=True)
    i1 = jnp.min(jnp.where(logits == v1, lanef, float(LANES)), axis=-1, keepdims=True).astype(I32)
    rest = jnp.where(lane == i1, -jnp.inf, logits)
    v2 = jnp.max(rest, axis=-1, keepdims=True)
    i2 = jnp.min(jnp.where(rest == v2, lanef, float(LANES)), axis=-1, keepdims=True).astype(I32)
    e = jnp.exp(v2 - v1)
    g1 = 1.0 / (1.0 + e)
    g2 = e / (1.0 + e)
    oh1 = lane == i1
    oh2 = lane == i2
    oh = jnp.where(oh1 | oh2, 1.0, 0.0)
    r = lax.broadcasted_iota(I32, (tm, tm), 0)
    cidx = lax.broadcasted_iota(I32, (tm, tm), 1)
    tri = jnp.where(cidx < r, 1.0, 0.0).astype(BF16)
    cum = jnp.dot(tri, oh.astype(BF16), preferred_element_type=F32) + run_ref[...]
    r1 = jnp.sum(jnp.where(oh1, cum, 0.0), axis=-1, keepdims=True).astype(I32)
    r2 = jnp.sum(jnp.where(oh2, cum, 0.0), axis=-1, keepdims=True).astype(I32)
    run_ref[...] += jnp.sum(oh, axis=0, keepdims=True)
    meta_ref[...] = jnp.where(lane == 0, i1, jnp.where(lane == 1, i2,
                              jnp.where(lane == 2, r1, jnp.where(lane == 3, r2, 0))))
    gate_ref[...] = jnp.where(lane == 0, g1, jnp.where(lane == 1, g2, 0.0))
    cnt_ref[...] = run_ref[...]


def norm_router(x, g3, w_router_p, b_router_p, l, lr, n_exp, tm=256):
    S, D = x.shape
    tm = min(tm, S)
    return pl.pallas_call(
        functools.partial(_router_kernel, n_exp=n_exp),
        grid=(S // tm,),
        in_specs=[pl.BlockSpec((tm, D), lambda i: (i, 0)),
                  pl.BlockSpec((None, 1, D), lambda i: (l, 0, 0)),
                  pl.BlockSpec((None, D, LANES), lambda i: (lr, 0, 0)),
                  pl.BlockSpec((None, 1, LANES), lambda i: (lr, 0, 0))],
        out_specs=[pl.BlockSpec((tm, D), lambda i: (i, 0)),
                   pl.BlockSpec((tm, LANES), lambda i: (i, 0)),
                   pl.BlockSpec((tm, LANES), lambda i: (i, 0)),
                   pl.BlockSpec((1, LANES), lambda i: (0, 0))],
        out_shape=[jax.ShapeDtypeStruct((S, D), F32), jax.ShapeDtypeStruct((S, LANES), I32),
                   jax.ShapeDtypeStruct((S, LANES), F32), jax.ShapeDtypeStruct((1, LANES), F32)],
        scratch_shapes=[pltpu.VMEM((1, LANES), F32)],
        compiler_params=_cparams(1, 40),
        name="norm_router",
    )(x, g3, w_router_p, b_router_p)


def _row_copy(src_hbm, dst, src_row, dst_row, sem):
    return pltpu.make_async_copy(src_hbm.at[pl.ds(src_row, 1)], dst.at[pl.ds(dst_row, 1)], sem)


def _gather_kernel(order_ref, nv_ref, h_hbm, xs_ref, buf, sem, *, tg):
    i = pl.program_id(0)
    nv = nv_ref[0]
    slot = i % 2

    def start_tile(t, s):
        def body(it, carry):
            for q in range(2):
                r = 2 * it + q
                _row_copy(h_hbm, buf.at[s], order_ref[t * tg + r], r, sem.at[s]).start(priority=q)
            return carry
        lax.fori_loop(0, tg // 2, body, 0, unroll=4)

    def wait_tile(s):
        def body(r, carry):
            _row_copy(h_hbm, buf.at[s], 0, r, sem.at[s]).wait()
            return carry
        lax.fori_loop(0, tg, body, 0, unroll=8)

    @pl.when(i == 0)
    def _():
        start_tile(0, 0)

    @pl.when(i + 1 < nv)
    def _():
        start_tile(i + 1, 1 - slot)

    @pl.when(i < nv)
    def _():
        wait_tile(slot)
        xs_ref[...] = buf[slot].astype(BF16)

    @pl.when(i >= nv)
    def _():
        xs_ref[...] = jnp.zeros_like(xs_ref)


def gather_rows(h, order, nvalid, n_tiles, tg):
    D = h.shape[1]
    grid_spec = pltpu.PrefetchScalarGridSpec(
        num_scalar_prefetch=2,
        grid=(n_tiles,),
        in_specs=[pl.BlockSpec(memory_space=pl.ANY)],
        out_specs=pl.BlockSpec((tg, D), lambda i, order, nv: (i, 0)),
        scratch_shapes=[pltpu.VMEM((2, tg, D), F32), pltpu.SemaphoreType.DMA((2,))],
    )
    return pl.pallas_call(
        functools.partial(_gather_kernel, tg=tg),
        grid_spec=grid_spec,
        out_shape=jax.ShapeDtypeStruct((n_tiles * tg, D), BF16),
        compiler_params=_cparams(1, 40),
        name="moe_gather",
    )(order, nvalid, h)


def _gmm_kernel(te_ref, src_ref, first_ref, nxt_ref, lastrun_ref, half_ref, nv_ref, *refs, nw, epi, lr, tn):
    x_ref = refs[0]
    w_hbm = refs[1:1 + nw]
    out_ref = refs[1 + nw]
    stage, wbf, sem = refs[2 + nw:5 + nw]
    n = pl.program_id(0)
    i = pl.program_id(1)

    def w_copy(j, e, nb):
        return pltpu.make_async_copy(w_hbm[j].at[lr, e, :, pl.ds(nb * tn, tn)], stage.at[j], sem.at[j])

    @pl.when((n == 0) & (i == 0))
    def _():
        for j in range(nw):
            w_copy(j, te_ref[0], 0).start()

    @pl.when(first_ref[i] == 1)
    def _():
        for j in range(nw):
            w_copy(j, te_ref[i], n).wait()
        _cast_blocks(stage, wbf, nw)
        nb_next = n + lastrun_ref[i]

        @pl.when(nb_next < pl.num_programs(0))
        def _():
            for j in range(nw):
                w_copy(j, nxt_ref[i], nb_next).start()

    def compute(rows):
        x = x_ref[:rows]
        accs = [jnp.dot(x, wbf[j], preferred_element_type=F32) for j in range(nw)]
        if epi == "swiglu":
            g, u = accs
            out = g * _sigmoid(g) * u
        else:
            out = accs[0]
        out_ref[:rows] = out.astype(out_ref.dtype)

    tg = x_ref.shape[0]
    valid = i < nv_ref[0]
    half = half_ref[i] == 1

    @pl.when(valid & jnp.logical_not(half))
    def _():
        compute(tg)

    @pl.when(valid & half)
    def _():
        compute(tg // 2)
        out_ref[tg // 2:] = jnp.zeros((tg - tg // 2, out_ref.shape[1]), out_ref.dtype)

    @pl.when(jnp.logical_not(valid))
    def _():
        out_ref[...] = jnp.zeros_like(out_ref)


def grouped_matmul(xs, ws, lr, tile_meta, *, epi, out_dtype, tg, tn, name):
    P, K = xs.shape
    N = ws[0].shape[-1]
    nw = len(ws)
    tn = min(tn, N)
    assert N % tn == 0 and P % tg == 0
    grid_spec = pltpu.PrefetchScalarGridSpec(
        num_scalar_prefetch=len(tile_meta),
        grid=(N // tn, P // tg),
        in_specs=[pl.BlockSpec((tg, K), lambda n, i, te, src, *_: (src[i], 0))]
        + [pl.BlockSpec(memory_space=pl.ANY)] * nw,
        out_specs=pl.BlockSpec((tg, tn), lambda n, i, *_: (i, n)),
        scratch_shapes=[pltpu.VMEM((nw, K, tn), F32), pltpu.VMEM((nw, K, tn), BF16),
                        pltpu.SemaphoreType.DMA((nw,))],
    )
    obytes = jnp.dtype(out_dtype).itemsize
    vmem = 2 * tg * K * 2 + nw * K * tn * 6 + 2 * tg * tn * obytes + (nw + 1) * tg * tn * 4
    return pl.pallas_call(
        functools.partial(_gmm_kernel, nw=nw, epi=epi, lr=lr, tn=tn),
        grid_spec=grid_spec,
        out_shape=jax.ShapeDtypeStruct((P, N), out_dtype),
        compiler_params=_cparams(2, vmem // MIB + VMEM_SLACK_MIB),
        name=name,
    )(*tile_meta, xs, *ws)


def _combine_kernel(dest_ref, x_ref, gate_ref, ys_hbm, out_ref, buf, sem, *, tc, top_k):
    i = pl.program_id(0)
    slot = i % 2

    def start_tile(t, s):
        def body(r, carry):
            for k in range(top_k):
                _row_copy(ys_hbm, buf.at[s, k], dest_ref[(t * tc + r) * top_k + k], r,
                          sem.at[s]).start(priority=k % 2)
            return carry
        lax.fori_loop(0, tc, body, 0, unroll=4)

    def wait_tile(s):
        def body(r, carry):
            for k in range(top_k):
                _row_copy(ys_hbm, buf.at[s, k], 0, r, sem.at[s]).wait()
            return carry
        lax.fori_loop(0, tc, body, 0, unroll=4)

    @pl.when(i == 0)
    def _():
        start_tile(0, 0)

    @pl.when(i + 1 < pl.num_programs(0))
    def _():
        start_tile(i + 1, 1 - slot)

    wait_tile(slot)
    g = gate_ref[...]
    out = x_ref[...]
    for k in range(top_k):
        out = out + g[:, k:k + 1] * buf[slot, k]
    out_ref[...] = out


def moe_combine(x, gates, ys, dest_flat, top_k, tc=256):
    S, D = x.shape
    tc = min(tc, S)
    grid_spec = pltpu.PrefetchScalarGridSpec(
        num_scalar_prefetch=1,
        grid=(S // tc,),
        in_specs=[pl.BlockSpec((tc, D), lambda i, d: (i, 0)),
                  pl.BlockSpec((tc, LANES), lambda i, d: (i, 0)),
                  pl.BlockSpec(memory_space=pl.ANY)],
        out_specs=pl.BlockSpec((tc, D), lambda i, d: (i, 0)),
        scratch_shapes=[pltpu.VMEM((2, top_k, tc, D), F32), pltpu.SemaphoreType.DMA((2,))],
    )
    return pl.pallas_call(
        functools.partial(_combine_kernel, tc=tc, top_k=top_k),
        grid_spec=grid_spec,
        out_shape=jax.ShapeDtypeStruct((S, D), F32),
        compiler_params=_cparams(1, 48),
        name="moe_combine",
    )(dest_flat, x, gates, ys)


def moe_layer(x, norm_ffn3, l, w_router, b_router, w_e_gate, w_e_up, w_e_down, lr, cfg):
    S, D = x.shape
    E, top_k, tg = cfg.n_experts, 2, cfg.tg
    tg = min(tg, S)
    wr = jnp.pad(w_router, ((0, 0), (0, 0), (0, LANES - E)))
    br = jnp.pad(b_router, ((0, 0), (0, LANES - E)))[:, None, :]
    h, meta, gates, cnt = norm_router(x, norm_ffn3, wr, br, l, lr, E)

    eid, rank = meta[:, 0:top_k], meta[:, top_k:2 * top_k]
    counts = cnt[0, :E].astype(I32)
    padded = (counts + tg - 1) // tg * tg
    ends = jnp.cumsum(padded)
    dest = (ends - padded)[eid] + rank
    n_tiles = (S * top_k) // tg + E
    order = jnp.zeros((n_tiles * tg,), I32).at[dest.reshape(-1)].set(
        jnp.repeat(jnp.arange(S, dtype=I32), top_k))
    nvalid = (ends[-1] // tg).astype(I32)
    tile = jnp.arange(n_tiles, dtype=I32)
    src = jnp.minimum(tile, nvalid - 1)
    te = jnp.sum((src * tg)[:, None] >= ends[None, :], axis=1).astype(I32)
    first = ((tile == 0) | (te != jnp.roll(te, 1))).astype(I32)
    later = (tile[None, :] > tile[:, None]) & (te[None, :] != te[:, None])
    nxt_idx = jnp.min(jnp.where(later, tile[None, :], n_tiles), axis=1)
    lastrun = (nxt_idx == n_tiles).astype(I32)
    nxt = jnp.where(lastrun == 1, te[0], te[jnp.minimum(nxt_idx, n_tiles - 1)]).astype(I32)
    tile_rows = jnp.clip((counts + ends - padded)[te] - src * tg, 0, tg)
    half = (tile_rows <= tg // 2).astype(I32)
    nv = nvalid.reshape(1)
    tile_meta = (te, src, first, nxt, lastrun, half, nv)

    xs = gather_rows(h, order, nv, n_tiles, tg)
    a = grouped_matmul(xs, [w_e_gate, w_e_up], lr, tile_meta, epi="swiglu", out_dtype=BF16,
                       tg=tg, tn=cfg.tn_b, name="moe_gate_up")
    ys = grouped_matmul(a, [w_e_down], lr, tile_meta, epi="plain", out_dtype=F32,
                        tg=tg, tn=cfg.tn_gd, name="moe_down")
    return moe_combine(x, gates, ys, dest.reshape(-1), top_k)


def _forward(x, mem, rel_bias, norm_mix, norm_ffn, norm_mem, w_in, qn_a, kn_a, conv_w, conv_b,
             gate_bias_b, hnorm_b, w_mem_kv, qn_m, kn_m, w_br_a, w_br_b, w_br_m, w_out,
             w_ff_gate, w_ff_up, w_ff_down, w_router, b_router, w_e_gate, w_e_up, w_e_down, cfg):
    B, S, D = x.shape
    assert B == 1 and mem.shape[0] == 1
    depth = norm_mix.shape[0]
    ng = len(cfg.a_groups)
    w_a = ng * cfg.heads_per_group * cfg.hd_a
    w_b = cfg.h_b * cfg.hd_b
    w_m = cfg.h_m * cfg.hd_m
    if_col0 = 3 * w_a + 3 * w_b
    tail_col0 = if_col0 + 2 * cfg.h_b
    d_ff = w_ff_gate.shape[-1]
    tk_down = min(cfg.tk_down, _round_up(d_ff, LANES))
    d_ff_p = _round_up(d_ff, tk_down)

    def row3(p):
        return p[:, None, :]

    x = x.reshape(S, D)
    mem2 = mem.reshape(mem.shape[1], D)
    norm_mix3, norm_ffn3, norm_mem3 = row3(norm_mix), row3(norm_ffn), row3(norm_mem)
    qn_a3, kn_a3, qn_m3, kn_m3 = row3(qn_a), row3(kn_a), row3(qn_m), row3(kn_m)
    conv_b3, hnorm3 = row3(conv_b), row3(hnorm_b)
    gbias3 = row3(jnp.pad(gate_bias_b, ((0, 0), (0, LANES - 2 * cfg.h_b))))
    mm = functools.partial(matmul, tm=cfg.tm)
    gw = cfg.heads_per_group * cfg.hd_a
    w_in_t = jnp.swapaxes(w_in, 1, 2)

    for l in range(depth):
        h = rmsnorm(x, norm_mix3, l)
        proj_g = [mm(h, [w_in_t], lead=l, trans=True, n_out=3 * gw, tn=gw, phases=cfg.a_groups[g][1],
                     blk_of=lambda n, g=g: n * ng + g, name=f"proj_a{g}").reshape(
                         cfg.a_groups[g][1], S // cfg.a_groups[g][1], 3 * gw)
                  for g in range(ng)]
        proj_b = mm(h, [w_in_t], lead=l, trans=True, col0=3 * w_a, n_out=3 * w_b, tn=cfg.tn_b,
                    name="proj_b")
        gates_b = mm(h, [w_in_t], lead=l, trans=True, col0=if_col0, n_out=LANES, out_dtype=F32,
                     tn=LANES, name="proj_if")
        proj_t = mm(h, [w_in_t], lead=l, trans=True, col0=tail_col0, n_out=w_b + w_m, tn=cfg.tn_b,
                    name="proj_tail")
        proj_gt = mm(h, [w_in_t], lead=l, trans=True, col0=tail_col0 + w_b + w_m, n_out=3 * D,
                     tn=cfg.tn, name="proj_gates")

        outs, lses = zip(*[band_attention(proj_g[g], rel_bias, qn_a3, kn_a3, l, g, cfg)
                           for g in range(ng)])
        y_a = alpha_merge(outs, lses)
        y_b = mlstm(proj_b, gates_b, proj_t, conv_w, conv_b3, gbias3, hnorm3, l, cfg)
        hm = rmsnorm(mem2, norm_mem3, l)
        kv = mm(hm, [w_mem_kv], lead=l, n_out=2 * w_m, tn=cfg.tn, name="mem_kv")
        y_m = cross_attention(proj_t, w_b, kv, qn_m3, kn_m3, l, cfg)
        y = gated_merge(y_a, y_b, y_m, w_br_a, w_br_b, w_br_m, proj_gt, 0, l, D,
                        tm=cfg.tm // 2, tn=cfg.tn)
        x = mm(y, [w_out], lead=l, n_out=D, epi="residual", res=x, out_dtype=F32, tn=cfg.tn_b,
               name="out_proj")

        if l % 2 == 0:
            ld = l // 2
            h2 = rmsnorm(x, norm_ffn3, l)
            a = matmul(h2, [w_ff_gate, w_ff_up], lead=ld, n_out=d_ff_p, epi="swiglu", n_valid=d_ff,
                       tm=cfg.tm // 2, tn=cfg.tn2, name="ffn_gate_up")
            wd = jnp.pad(w_ff_down[ld], ((0, d_ff_p - d_ff), (0, 0))).astype(BF16)
            x = matmul_ktiled_residual(a, wd, x, tm=cfg.tm, tn=cfg.tn_down, tk=tk_down, name="ffn_down")
        else:
            x = moe_layer(x, norm_ffn3, l, w_router, b_router, w_e_gate, w_e_up, w_e_down, l // 2, cfg)
    return x.reshape(B, S, D)


def kernel(x, mem, rel_bias, norm_mix, norm_ffn, norm_mem, w_in, qn_a, kn_a, conv_w, conv_b, gate_bias_b, hnorm_b, w_mem_kv, qn_m, kn_m, w_br_a, w_br_b, w_br_m, w_out, w_ff_gate, w_ff_up, w_ff_down, w_router, b_router, w_e_gate, w_e_up, w_e_down):
    return _forward(x, mem, rel_bias, norm_mix, norm_ffn, norm_mem, w_in, qn_a, kn_a, conv_w, conv_b,
                    gate_bias_b, hnorm_b, w_mem_kv, qn_m, kn_m, w_br_a, w_br_b, w_br_m, w_out,
                    w_ff_gate, w_ff_up, w_ff_down, w_router, b_router, w_e_gate, w_e_up, w_e_down,
                    Cfg())
```

```python
import functools
from typing import NamedTuple

import numpy as np
import jax
import jax.numpy as jnp
from jax import lax
from jax.experimental import pallas as pl
from jax.experimental.pallas import tpu as pltpu

F32 = jnp.float32
BF16 = jnp.bfloat16
I32 = jnp.int32
EPS = 1e-6
NEG = -1e30
MIB = 1 << 20
LANES = 128
SUBLANES = 8
VMEM_CAP_MIB = 60
VMEM_SLACK_MIB = 12


class Cfg(NamedTuple):
    a_groups: tuple = ((128, 1), (512, 4), (2048, 16))
    heads_per_group: int = 4
    hd_a: int = 128
    band_block: int = 128
    h_b: int = 4
    hd_b: int = 384
    chunk: int = 128
    conv_w: int = 4
    h_m: int = 4
    hd_m: int = 256
    n_buckets: int = 32
    max_dist: int = 2048
    n_experts: int = 8
    tm: int = 1024
    tn: int = 1024
    tn_b: int = 512
    tn2: int = 512
    tn_gd: int = 1024
    tg: int = 512
    tk_down: int = 5504
    tn_down: int = 512


def _cparams(n_axes, vmem_mib):
    return pltpu.CompilerParams(dimension_semantics=("arbitrary",) * n_axes,
                                vmem_limit_bytes=int(min(vmem_mib, VMEM_CAP_MIB)) * MIB)


def _round_up(a, b):
    return -(-a // b) * b


def _sigmoid(x):
    return 1.0 / (1.0 + jnp.exp(-x))


def _rms(x, g):
    return x * lax.rsqrt(jnp.mean(x * x, axis=-1, keepdims=True) + EPS) * g


def _rmsnorm_kernel(x_ref, g_ref, o_ref):
    o_ref[...] = _rms(x_ref[...].astype(F32), g_ref[...]).astype(o_ref.dtype)


def rmsnorm(x, g3, l, out_dtype=BF16, tm=256):
    M, D = x.shape
    tm = min(tm, M)
    return pl.pallas_call(
        _rmsnorm_kernel,
        grid=(M // tm,),
        in_specs=[pl.BlockSpec((tm, D), lambda i: (i, 0)),
                  pl.BlockSpec((None, 1, D), lambda i: (l, 0, 0))],
        out_specs=pl.BlockSpec((tm, D), lambda i: (i, 0)),
        out_shape=jax.ShapeDtypeStruct((M, D), out_dtype),
        compiler_params=_cparams(1, 32),
        name="rmsnorm",
    )(x, g3)


def _cast_blocks(stage, wbf, nw, rows=512):
    n_rows = stage.shape[1]
    for j in range(nw):
        for r in range(0, n_rows, rows):
            sl = slice(r, min(r + rows, n_rows))
            wbf[j, sl] = stage[j, sl].astype(BF16)


def _mm_kernel(lhs_ref, *refs, nw, trans, epi, n_valid, tn, gn, last_w, src_of, phases):
    w_hbm = refs[:nw]
    pos = nw
    res_ref = None
    if epi == "residual":
        res_ref = refs[pos]
        pos += 1
    out_ref = refs[pos]
    stage, wbf, sem = refs[pos + 1:pos + 4]
    n = pl.program_id(0)
    m = pl.program_id(1)

    def copies(nb, width):
        out = []
        for j in range(nw):
            if width == tn:
                dst = stage.at[j]
            elif trans:
                dst = stage.at[j, pl.ds(0, width), :]
            else:
                dst = stage.at[j, :, pl.ds(0, width)]
            out.append(pltpu.make_async_copy(src_of(w_hbm[j], nb, width), dst, sem.at[j]))
        return out

    def for_block(nb, fn):
        if last_w == tn:
            for c in copies(nb, tn):
                fn(c)
        elif isinstance(nb, int):
            for c in copies(nb, last_w if nb == gn - 1 else tn):
                fn(c)
        else:
            @pl.when(nb == gn - 1)
            def _():
                for c in copies(nb, last_w):
                    fn(c)

            @pl.when(nb != gn - 1)
            def _():
                for c in copies(nb, tn):
                    fn(c)

    @pl.when((n == 0) & (m == 0))
    def _():
        for_block(0, lambda c: c.start())

    @pl.when(m == 0)
    def _():
        for_block(n, lambda c: c.wait())
        _cast_blocks(stage, wbf, nw)

        @pl.when(n + 1 < gn)
        def _():
            for_block(n + 1, lambda c: c.start())

    dims = (((1,), (1,)), ((), ())) if trans else (((1,), (0,)), ((), ()))
    accs = [lax.dot_general(lhs_ref[...], wbf[j], dims, preferred_element_type=F32) for j in range(nw)]
    if epi == "plain":
        out = accs[0]
    elif epi == "residual":
        out = res_ref[...] + accs[0]
    else:
        g, u = accs
        out = g * _sigmoid(g) * u
        if n_valid is not None:
            col = pl.program_id(0) * tn + lax.broadcasted_iota(I32, out.shape, 1)
            out = jnp.where(col < n_valid, out, 0.0)
    if phases == 1:
        out_ref[...] = out.astype(out_ref.dtype)
    else:
        pm_ref = refs[pos + 4]
        tm = out.shape[0]
        for s in range(tn // LANES):
            pm_ref[s] = out[:, s * LANES:(s + 1) * LANES]
        for p in range(phases):
            for s in range(tn // LANES):
                out_ref[p, :, s * LANES:(s + 1) * LANES] = pm_ref[
                    s, pl.ds(p, tm // phases, stride=phases), :].astype(out_ref.dtype)


def matmul(lhs, ws, *, lead=None, col0=0, n_out, epi="plain", res=None, n_valid=None, phases=1,
           trans=False, blk_of=None, out_dtype=BF16, tm=1024, tn=512, name="matmul"):
    M, K = lhs.shape
    nw = len(ws)
    tm = min(tm, M)
    tn = min(tn, _round_up(n_out, LANES))
    assert tn % LANES == 0 and M % tm == 0 and ws[0].dtype == F32
    gn = pl.cdiv(n_out, tn)
    last_w = n_out - (gn - 1) * tn
    if blk_of is None:
        def col_of(nb):
            return col0 + nb * tn
    else:
        def col_of(nb):
            return blk_of(nb) * tn
    if trans:
        n_rows = ws[0].shape[1]
        assert lead is not None and n_rows % SUBLANES == 0 and col0 % SUBLANES == 0 and last_w % SUBLANES == 0
        ws = [w.reshape(-1, K) for w in ws]
        w_block = (tn, K)

        def src_of(w, nb, width):
            return w.at[pl.ds(pl.multiple_of(lead * n_rows + col_of(nb), SUBLANES), width), :]
    else:
        n_cols = ws[0].shape[-1]
        assert col0 % LANES == 0 and last_w % LANES == 0 and n_cols % LANES == 0
        assert col0 + n_out <= n_cols or n_valid is not None
        w_block = (K, tn)

        def src_of(w, nb, width):
            w = w if lead is None else w.at[lead]
            col = jnp.minimum(col_of(nb), n_cols - width)
            return w.at[:, pl.ds(pl.multiple_of(col, LANES), width)]
    in_specs = [pl.BlockSpec((tm, K), lambda n, m: (m, 0))] + [pl.BlockSpec(memory_space=pl.ANY)] * nw
    args = [lhs] + list(ws)
    if epi == "residual":
        in_specs.append(pl.BlockSpec((tm, tn), lambda n, m: (m, n)))
        args.append(res)
    obytes = jnp.dtype(out_dtype).itemsize
    vmem = (3 * tm * K * 2 + nw * K * tn * 6 + 2 * tm * tn * obytes
            + (2 * tm * tn * 4 if epi == "residual" else 0) + (nw + 1 + (phases > 1)) * tm * tn * 4)
    scratch = [pltpu.VMEM((nw,) + w_block, F32), pltpu.VMEM((nw,) + w_block, BF16),
               pltpu.SemaphoreType.DMA((nw,))]
    if phases == 1:
        out_spec = pl.BlockSpec((tm, tn), lambda n, m: (m, n))
        out_shape = jax.ShapeDtypeStruct((M, n_out), out_dtype)
    else:
        assert tm % (phases * 2 * SUBLANES) == 0 and last_w == tn and epi == "plain"
        out_spec = pl.BlockSpec((phases, tm // phases, tn), lambda n, m: (0, m, n))
        out_shape = jax.ShapeDtypeStruct((phases, M // phases, n_out), out_dtype)
        scratch.append(pltpu.VMEM((tn // LANES, tm, LANES), F32))
    return pl.pallas_call(
        functools.partial(_mm_kernel, nw=nw, trans=trans, epi=epi, n_valid=n_valid, tn=tn, gn=gn,
                          last_w=last_w, src_of=src_of, phases=phases),
        grid=(gn, M // tm),
        in_specs=in_specs,
        out_specs=out_spec,
        out_shape=out_shape,
        scratch_shapes=scratch,
        compiler_params=_cparams(2, vmem // MIB + VMEM_SLACK_MIB),
        name=name,
    )(*args)


def _mmk_kernel(lhs_ref, w_ref, res_ref, out_ref, acc_ref):
    k = pl.program_id(2)

    @pl.when(k == 0)
    def _():
        acc_ref[...] = jnp.zeros_like(acc_ref)

    acc_ref[...] += jnp.dot(lhs_ref[...], w_ref[...], preferred_element_type=F32)

    @pl.when(k == pl.num_programs(2) - 1)
    def _():
        out_ref[...] = res_ref[...] + acc_ref[...]


def matmul_ktiled_residual(lhs, w, res, *, tm=1024, tn=1024, tk=1024, name="matmul_k"):
    M, K = lhs.shape
    N = w.shape[1]
    tm, tn, tk = min(tm, M), min(tn, N), min(tk, K)
    assert M % tm == 0 and N % tn == 0 and K % tk == 0
    vmem = 2 * tm * tk * 2 + 2 * tk * tn * 2 + 5 * tm * tn * 4
    return pl.pallas_call(
        _mmk_kernel,
        grid=(M // tm, N // tn, K // tk),
        in_specs=[pl.BlockSpec((tm, tk), lambda m, n, k: (m, k)),
                  pl.BlockSpec((tk, tn), lambda m, n, k: (k, n)),
                  pl.BlockSpec((tm, tn), lambda m, n, k: (m, n))],
        out_specs=pl.BlockSpec((tm, tn), lambda m, n, k: (m, n)),
        out_shape=jax.ShapeDtypeStruct((M, N), F32),
        scratch_shapes=[pltpu.VMEM((tm, tn), F32)],
        compiler_params=_cparams(3, vmem // MIB + VMEM_SLACK_MIB),
        name=name,
    )(lhs, w, res)


def _t5_bucket_np(dist, n_buckets, max_dist):
    max_exact = n_buckets // 2
    d = np.maximum(dist, 1).astype(np.float32)
    large = max_exact + (np.log(d / np.float32(max_exact)) / np.float32(np.log(max_dist / max_exact))
                         * np.float32(n_buckets - max_exact)).astype(np.int32)
    large = np.minimum(large, n_buckets - 1)
    return np.where(dist < max_exact, dist, large).astype(np.int32)


def _band_kernel(tab_ref, bkt_ref, q_ref, kp_ref, kc_ref, vp_ref, vc_ref, qn_ref, kn_ref,
                 o_ref, lse_ref, bias_ref, *, hpg, hd, head0, buckets, scale):
    p = pl.program_id(0)
    n = pl.program_id(1)
    bb = kp_ref.shape[0]
    qb = q_ref.shape[0] // bb

    @pl.when((p == 0) & (n == 0))
    def _():
        bkt = bkt_ref[...]
        for h in range(hpg):
            acc = jnp.full((bb, 2 * bb), NEG, F32)
            for b in buckets:
                acc = jnp.where(bkt == b, tab_ref[b, head0 + h], acc)
            bias_ref[h] = acc

    ki = lax.broadcasted_iota(I32, (bb, 2 * bb), 1)
    kvalid = (ki >= bb) | (n > 0)
    for h in range(hpg):
        sl = slice(h * hd, (h + 1) * hd)
        q_all = (_rms(q_ref[:, sl].astype(F32), qn_ref[...]) * scale).astype(BF16)
        k_all = jnp.concatenate([kp_ref[:, sl], kc_ref[:, sl]], axis=0).astype(F32)
        k_all = _rms(k_all, kn_ref[...]).astype(BF16)
        v_all = jnp.concatenate([vp_ref[:, sl], vc_ref[:, sl]], axis=0)
        for j in range(qb):
            rows = slice(j * bb, (j + 1) * bb)
            keys = slice(j * bb, (j + 2) * bb)
            s = lax.dot_general(q_all[rows], k_all[keys], (((1,), (1,)), ((), ())),
                                preferred_element_type=F32) + bias_ref[h]
            if j == 0:
                s = jnp.where(kvalid, s, NEG)
            m = jnp.max(s, axis=-1, keepdims=True)
            e = jnp.exp(s - m)
            l = jnp.sum(e, axis=-1, keepdims=True)
            o = jnp.dot((e / l).astype(BF16), v_all[keys], preferred_element_type=F32)
            o_ref[rows, sl] = o.astype(o_ref.dtype)
            lse_ref[rows, sl] = jnp.broadcast_to(m + jnp.log(l), (bb, hd))


def band_attention(x, rel_bias, qn3, kn3, l, g, cfg):
    win, dil = cfg.a_groups[g]
    assert x.shape[0] == dil
    _, ls, npa = x.shape
    steps = win // dil
    bb, hpg, hd = cfg.band_block, cfg.heads_per_group, cfg.hd_a
    gw = hpg * hd
    assert ls % bb == 0 and steps <= bb and npa == 3 * gw
    nblk = ls // bb

    qi = np.arange(bb)[:, None]
    ki = np.arange(2 * bb)[None, :]
    rel = qi + bb - ki
    inside = (rel >= 0) & (rel <= steps)
    bkt = np.where(inside, _t5_bucket_np(np.maximum(rel, 0) * dil, cfg.n_buckets, cfg.max_dist), -1)
    buckets = tuple(int(b) for b in np.unique(bkt[inside]))

    qb = max(d for d in (4, 2, 1) if nblk % d == 0)

    def cur(col):
        return pl.BlockSpec((None, qb * bb, gw), lambda p, n: (p, n, col))

    def prev(col):
        return pl.BlockSpec((None, bb, gw), lambda p, n: (p, jnp.maximum(n * qb - 1, 0), col))

    return pl.pallas_call(
        functools.partial(_band_kernel, hpg=hpg, hd=hd, head0=g * hpg, buckets=buckets,
                          scale=float(hd) ** -0.5),
        grid=(dil, nblk // qb),
        in_specs=[pl.BlockSpec(memory_space=pltpu.SMEM),
                  pl.BlockSpec((bb, 2 * bb), lambda p, n: (0, 0)),
                  cur(0), prev(1), cur(1), prev(2), cur(2),
                  pl.BlockSpec((None, 1, hd), lambda p, n: (l, 0, 0)),
                  pl.BlockSpec((None, 1, hd), lambda p, n: (l, 0, 0))],
        out_specs=[cur(0), cur(0)],
        out_shape=[jax.ShapeDtypeStruct((dil, ls, gw), BF16),
                   jax.ShapeDtypeStruct((dil, ls, gw), F32)],
        scratch_shapes=[pltpu.VMEM((hpg, bb, 2 * bb), F32)],
        compiler_params=_cparams(2, 32),
        name=f"band_attn_g{g}",
    )(rel_bias, jnp.asarray(bkt, I32), x, x, x, x, x, qn3, kn3)


def _alpha_kernel(*refs, dils, gw):
    ng = len(dils)
    o_refs, l_refs, y_ref = refs[:ng], refs[ng:2 * ng], refs[2 * ng]
    scr = list(refs[2 * ng + 1:])
    tm = y_ref.shape[0]

    def token_order(ref, dil):
        if dil == 1:
            return ref[0].astype(F32)
        buf = scr.pop(0)
        for p in range(dil):
            for s in range(gw // LANES):
                buf[s, pl.ds(p, tm // dil, stride=dil), :] = ref[p, :, s * LANES:(s + 1) * LANES].astype(F32)
        return jnp.concatenate([buf[s] for s in range(gw // LANES)], axis=1)

    ls = [token_order(r, d) for r, d in zip(l_refs, dils)]
    m = functools.reduce(jnp.maximum, ls)
    es = [jnp.exp(v - m) for v in ls]
    den = functools.reduce(lambda a, b: a + b, es)
    for g in range(ng):
        o = token_order(o_refs[g], dils[g])
        y_ref[:, g * gw:(g + 1) * gw] = (es[g] / den * o).astype(y_ref.dtype)


def alpha_merge(outs, lses, tm=512):
    ng = len(outs)
    dils = tuple(o.shape[0] for o in outs)
    gw = outs[0].shape[2]
    S = outs[0].shape[0] * outs[0].shape[1]
    tm = min(tm, S)
    specs = [pl.BlockSpec((d, tm // d, gw), lambda i: (0, i, 0)) for d in dils]
    n_scr = 2 * sum(d > 1 for d in dils)
    return pl.pallas_call(
        functools.partial(_alpha_kernel, dils=dils, gw=gw),
        grid=(S // tm,),
        in_specs=specs + specs,
        out_specs=pl.BlockSpec((tm, ng * gw), lambda i: (i, 0)),
        out_shape=jax.ShapeDtypeStruct((S, ng * gw), BF16),
        scratch_shapes=[pltpu.VMEM((gw // LANES, tm, LANES), F32)] * n_scr,
        compiler_params=_cparams(1, 40),
        name="alpha_merge",
    )(*outs, *lses)


def _cross_kernel(q_ref, k_ref, v_ref, qn_ref, kn_ref, o_ref, *, scale):
    q = _rms(q_ref[...].astype(F32), qn_ref[...]) * scale
    k = _rms(k_ref[...].astype(F32), kn_ref[...])
    s = lax.dot_general(q.astype(BF16), k.astype(BF16), (((1,), (1,)), ((), ())),
                        preferred_element_type=F32)
    m = jnp.max(s, axis=-1, keepdims=True)
    e = jnp.exp(s - m)
    pr = e / jnp.sum(e, axis=-1, keepdims=True)
    o_ref[...] = jnp.dot(pr.astype(BF16), v_ref[...], preferred_element_type=F32).astype(o_ref.dtype)


def cross_attention(proj_t, q_col0, kv, qn3, kn3, l, cfg, tm=1024):
    S = proj_t.shape[0]
    mlen = kv.shape[0]
    hm, hd = cfg.h_m, cfg.hd_m
    tm = min(tm, S)
    assert q_col0 % hd == 0
    qoff = q_col0 // hd
    return pl.pallas_call(
        functools.partial(_cross_kernel, scale=float(hd) ** -0.5),
        grid=(S // tm, hm),
        in_specs=[pl.BlockSpec((tm, hd), lambda i, h: (i, qoff + h)),
                  pl.BlockSpec((mlen, hd), lambda i, h: (0, h)),
                  pl.BlockSpec((mlen, hd), lambda i, h: (0, hm + h)),
                  pl.BlockSpec((None, 1, hd), lambda i, h: (l, 0, 0)),
                  pl.BlockSpec((None, 1, hd), lambda i, h: (l, 0, 0))],
        out_specs=pl.BlockSpec((tm, hd), lambda i, h: (i, h)),
        out_shape=jax.ShapeDtypeStruct((S, hm * hd), BF16),
        compiler_params=_cparams(2, 32),
        name="cross_attn",
    )(proj_t, kv, kv, qn3, kn3)


def _mlstm_kernel(q_ref, k_ref, v_ref, g_ref, ob_ref, cw_ref, cb_ref, gb_ref, hn_ref,
                  y_ref, cn_ref, ms_ref, xw_ref, *, nh, hd, dp, conv_w):
    c = pl.program_id(0)
    L = q_ref.shape[0]
    wb = nh * hd

    @pl.when(c == 0)
    def _():
        cn_ref[...] = jnp.zeros_like(cn_ref)
        ms_ref[...] = jnp.zeros_like(ms_ref)
        xw_ref[...] = jnp.zeros_like(xw_ref)

    r = lax.broadcasted_iota(I32, ((conv_w - 1) * L, 2 * L), 0)
    cidx = lax.broadcasted_iota(I32, ((conv_w - 1) * L, 2 * L), 1)
    shift_mat = jnp.where(cidx == L + r % L - (r // L + 1), 1.0, 0.0).astype(BF16)

    def conv_silu(x_ref, col0):
        cols = slice(col0, col0 + wb)
        xb = x_ref[...]
        shifted = jnp.dot(shift_mat, jnp.concatenate([xw_ref[:, cols], xb], axis=0),
                          preferred_element_type=F32)
        y = cb_ref[:, cols] + cw_ref[conv_w - 1:conv_w, cols] * xb.astype(F32)
        for s in range(1, conv_w):
            y = y + cw_ref[conv_w - 1 - s:conv_w - s, cols] * shifted[(s - 1) * L:s * L]
        xw_ref[:, cols] = xb
        return y * _sigmoid(y)

    qs = conv_silu(q_ref, 0)
    ks = conv_silu(k_ref, wb) * (float(hd) ** -0.5)

    G = g_ref[...] + gb_ref[...]
    lf = jnp.minimum(G, 0.0) - jnp.log(1.0 + jnp.exp(-jnp.abs(G)))
    row = lax.broadcasted_iota(I32, (L, LANES), 0)
    F = lf
    sh = 1
    while sh < L:
        F = F + jnp.where(row >= sh, pltpu.roll(F, sh, axis=0), 0.0)
        sh *= 2
    GT = G.T
    FT = F.T
    ti = lax.broadcasted_iota(I32, (L, L), 0)
    si = lax.broadcasted_iota(I32, (L, L), 1)
    causal = ti >= si
    ones_col = (lax.broadcasted_iota(I32, (L, dp - hd), 1) == 0).astype(BF16)
    ms = ms_ref[...]

    for h in range(nh):
        sl = slice(h * hd, (h + 1) * hd)
        li_c, F_c = G[:, h:h + 1], F[:, nh + h:nh + h + 1]
        li_r, F_r = GT[h:h + 1, :], FT[nh + h:nh + h + 1, :]
        F_last = F_c[L - 1:L, :]
        m_prev = ms[:, h:h + 1]
        a_r = F_last - F_r + li_r
        b = jnp.max(a_r, axis=-1, keepdims=True)
        ea_c = jnp.exp(F_last - F_c + li_c - b)
        logw = jnp.where(causal, F_c - F_r + li_r, NEG)
        m_intra = jnp.max(logw, axis=-1, keepdims=True)
        m_inter = F_c + m_prev
        m_t = jnp.maximum(m_inter, m_intra)
        q = qs[:, sl].astype(BF16)
        k = ks[:, sl]
        v_aug = jnp.concatenate([v_ref[:, sl], ones_col], axis=1)
        s = lax.dot_general(q, k.astype(BF16), (((1,), (1,)), ((), ())),
                            preferred_element_type=F32) * jnp.exp(logw - m_t)
        inter = jnp.exp(m_inter - m_t)
        cn = cn_ref[h]
        num = (jnp.dot(s.astype(BF16), v_aug, preferred_element_type=F32)
               + inter * jnp.dot(q, cn.astype(BF16), preferred_element_type=F32))
        den = num[:, hd:hd + 1]
        hv = num[:, :hd] / jnp.maximum(jnp.abs(den), jnp.exp(-m_t))
        m_new = jnp.maximum(F_last + m_prev, b)
        decay = jnp.exp(F_last + m_prev - m_new)
        inj = jnp.exp(b - m_new)
        kv = jnp.dot(((inj * ea_c) * k).T.astype(BF16), v_aug, preferred_element_type=F32)
        cn_ref[h] = decay * cn + kv
        ms_ref[:, h:h + 1] = m_new
        hb = _rms(hv, hn_ref[:, sl])
        y_ref[:, sl] = (_sigmoid(ob_ref[:, sl].astype(F32)) * hb).astype(y_ref.dtype)


def mlstm(proj_b, gates, proj_t, conv_w, conv_b3, gbias3, hnorm3, l, cfg):
    S = proj_b.shape[0]
    L, nh, hd = cfg.chunk, cfg.h_b, cfg.hd_b
    wb = nh * hd
    dp = _round_up(hd + 1, LANES)
    assert L == LANES and S % L == 0 and 2 * nh <= LANES and cfg.conv_w <= SUBLANES
    cw = conv_w.shape[1]
    return pl.pallas_call(
        functools.partial(_mlstm_kernel, nh=nh, hd=hd, dp=dp, conv_w=cw),
        grid=(S // L,),
        in_specs=[pl.BlockSpec((L, wb), lambda c: (c, 0)),
                  pl.BlockSpec((L, wb), lambda c: (c, 1)),
                  pl.BlockSpec((L, wb), lambda c: (c, 2)),
                  pl.BlockSpec((L, LANES), lambda c: (c, 0)),
                  pl.BlockSpec((L, wb), lambda c: (c, 0)),
                  pl.BlockSpec((None, cw, 2 * wb), lambda c: (l, 0, 0)),
                  pl.BlockSpec((None, 1, 2 * wb), lambda c: (l, 0, 0)),
                  pl.BlockSpec((None, 1, LANES), lambda c: (l, 0, 0)),
                  pl.BlockSpec((None, 1, wb), lambda c: (l, 0, 0))],
        out_specs=pl.BlockSpec((L, wb), lambda c: (c, 0)),
        out_shape=jax.ShapeDtypeStruct((S, wb), BF16),
        scratch_shapes=[pltpu.VMEM((nh, hd, dp), F32), pltpu.VMEM((1, LANES), F32),
                        pltpu.VMEM((L, 2 * wb), BF16)],
        compiler_params=_cparams(1, 40),
        name="mlstm",
    )(proj_b, proj_b, proj_b, gates, proj_t, conv_w, conv_b3, gbias3, hnorm3)


def _merge_kernel(*refs, lead, tn, gn):
    y_refs, g_refs, w_hbm = refs[0:3], refs[3:6], refs[6:9]
    o_ref = refs[9]
    stages, wbfs, sem = refs[10:13], refs[13:16], refs[16]
    n = pl.program_id(0)
    m = pl.program_id(1)

    def copies(nb):
        col = pl.multiple_of(nb * tn, LANES)
        return [pltpu.make_async_copy(w_hbm[j].at[lead, :, pl.ds(col, tn)], stages[j], sem.at[j])
                for j in range(3)]

    @pl.when((n == 0) & (m == 0))
    def _():
        for c in copies(0):
            c.start()

    @pl.when(m == 0)
    def _():
        for c in copies(n):
            c.wait()
        for j in range(3):
            k_rows = stages[j].shape[0]
            for r in range(0, k_rows, 512):
                sl = slice(r, min(r + 512, k_rows))
                wbfs[j][sl] = stages[j][sl].astype(BF16)

        @pl.when(n + 1 < gn)
        def _():
            for c in copies(n + 1):
                c.start()

    acc = None
    for j in range(3):
        term = _sigmoid(g_refs[j][...].astype(F32)) * jnp.dot(y_refs[j][...], wbfs[j][...],
                                                             preferred_element_type=F32)
        acc = term if acc is None else acc + term
    o_ref[...] = acc.astype(o_ref.dtype)


def gated_merge(ya, yb, ym, w_a, w_b, w_m, gates, gate_col0, l, d, tm=512, tn=1024):
    S = ya.shape[0]
    tm = min(tm, S)
    while gate_col0 % tn or d % tn:
        tn //= 2
    goff = gate_col0 // tn
    nd = d // tn
    ys, wts = (ya, yb, ym), (w_a, w_b, w_m)

    def lhs_spec(y):
        return pl.BlockSpec((tm, y.shape[1]), lambda n, m: (m, 0))

    def g_spec(j):
        return pl.BlockSpec((tm, tn), lambda n, m: (m, goff + j * nd + n))

    ksum = sum(y.shape[1] for y in ys)
    vmem = 3 * tm * ksum * 2 + ksum * tn * 6 + 8 * tm * tn * 2 + 5 * tm * tn * 4
    return pl.pallas_call(
        functools.partial(_merge_kernel, lead=l, tn=tn, gn=nd),
        grid=(nd, S // tm),
        in_specs=[lhs_spec(y) for y in ys] + [g_spec(j) for j in range(3)]
        + [pl.BlockSpec(memory_space=pl.ANY)] * 3,
        out_specs=pl.BlockSpec((tm, tn), lambda n, m: (m, n)),
        out_shape=jax.ShapeDtypeStruct((S, d), BF16),
        scratch_shapes=[pltpu.VMEM((w.shape[1], tn), F32) for w in wts]
        + [pltpu.VMEM((w.shape[1], tn), BF16) for w in wts] + [pltpu.SemaphoreType.DMA((3,))],
        compiler_params=_cparams(2, vmem // MIB + VMEM_SLACK_MIB),
        name="gated_merge",
    )(*ys, gates, gates, gates, *wts)


def _router_kernel(x_ref, g_ref, wr_ref, br_ref, h_ref, meta_ref, gate_ref, cnt_ref, run_ref, *, n_exp):
    i = pl.program_id(0)
    tm = x_ref.shape[0]

    @pl.when(i == 0)
    def _():
        run_ref[...] = jnp.zeros_like(run_ref)

    y = _rms(x_ref[...], g_ref[...])
    h_ref[...] = y
    logits = jnp.dot(y, wr_ref[...], preferred_element_type=F32,
                     precision=lax.Precision.HIGHEST) + br_ref[...]
    lane = lax.broadcasted_iota(I32, (tm, LANES), 1)
    lanef = lane.astype(F32)
    logits = jnp.where(lane < n_exp, logits, -jnp.inf)
    v1 = jnp.max(logits, axis=-1, keepdims=True)
    i1 = jnp.min(jnp.where(logits == v1, lanef, float(LANES)), axis=-1, keepdims=True).astype(I32)
    rest = jnp.where(lane == i1, -jnp.inf, logits)
    v2 = jnp.max(rest, axis=-1, keepdims=True)
    i2 = jnp.min(jnp.where(rest == v2, lanef, float(LANES)), axis=-1, keepdims=True).astype(I32)
    e = jnp.exp(v2 - v1)
    g1 = 1.0 / (1.0 + e)
    g2 = e / (1.0 + e)
    oh1 = lane == i1
    oh2 = lane == i2
    oh = jnp.where(oh1 | oh2, 1.0, 0.0)
    r = lax.broadcasted_iota(I32, (tm, tm), 0)
    cidx = lax.broadcasted_iota(I32, (tm, tm), 1)
    tri = jnp.where(cidx < r, 1.0, 0.0).astype(BF16)
    cum = jnp.dot(tri, oh.astype(BF16), preferred_element_type=F32) + run_ref[...]
    r1 = jnp.sum(jnp.where(oh1, cum, 0.0), axis=-1, keepdims=True).astype(I32)
    r2 = jnp.sum(jnp.where(oh2, cum, 0.0), axis=-1, keepdims=True).astype(I32)
    run_ref[...] += jnp.sum(oh, axis=0, keepdims=True)
    meta_ref[...] = jnp.where(lane == 0, i1, jnp.where(lane == 1, i2,
                              jnp.where(lane == 2, r1, jnp.where(lane == 3, r2, 0))))
    gate_ref[...] = jnp.where(lane == 0, g1, jnp.where(lane == 1, g2, 0.0))
    cnt_ref[...] = run_ref[...]


def norm_router(x, g3, w_router_p, b_router_p, l, lr, n_exp, tm=256):
    S, D = x.shape
    tm = min(tm, S)
    return pl.pallas_call(
        functools.partial(_router_kernel, n_exp=n_exp),
        grid=(S // tm,),
        in_specs=[pl.BlockSpec((tm, D), lambda i: (i, 0)),
                  pl.BlockSpec((None, 1, D), lambda i: (l, 0, 0)),
                  pl.BlockSpec((None, D, LANES), lambda i: (lr, 0, 0)),
                  pl.BlockSpec((None, 1, LANES), lambda i: (lr, 0, 0))],
        out_specs=[pl.BlockSpec((tm, D), lambda i: (i, 0)),
                   pl.BlockSpec((tm, LANES), lambda i: (i, 0)),
                   pl.BlockSpec((tm, LANES), lambda i: (i, 0)),
                   pl.BlockSpec((1, LANES), lambda i: (0, 0))],
        out_shape=[jax.ShapeDtypeStruct((S, D), F32), jax.ShapeDtypeStruct((S, LANES), I32),
                   jax.ShapeDtypeStruct((S, LANES), F32), jax.ShapeDtypeStruct((1, LANES), F32)],
        scratch_shapes=[pltpu.VMEM((1, LANES), F32)],
        compiler_params=_cparams(1, 40),
        name="norm_router",
    )(x, g3, w_router_p, b_router_p)


def _row_copy(src_hbm, dst, src_row, dst_row, sem):
    return pltpu.make_async_copy(src_hbm.at[pl.ds(src_row, 1)], dst.at[pl.ds(dst_row, 1)], sem)


def _gather_kernel(order_ref, nv_ref, h_hbm, xs_ref, buf, sem, *, tg):
    i = pl.program_id(0)
    nv = nv_ref[0]
    slot = i % 2

    def start_tile(t, s):
        def body(it, carry):
            for q in range(2):
                r = 2 * it + q
                _row_copy(h_hbm, buf.at[s], order_ref[t * tg + r], r, sem.at[s]).start(priority=q)
            return carry
        lax.fori_loop(0, tg // 2, body, 0, unroll=4)

    def wait_tile(s):
        pltpu.make_async_copy(h_hbm.at[pl.ds(0, tg)], buf.at[s], sem.at[s]).wait()

    @pl.when(i == 0)
    def _():
        start_tile(0, 0)

    @pl.when(i + 1 < nv)
    def _():
        start_tile(i + 1, 1 - slot)

    @pl.when(i < nv)
    def _():
        wait_tile(slot)
        xs_ref[...] = buf[slot].astype(BF16)

    @pl.when(i >= nv)
    def _():
        xs_ref[...] = jnp.zeros_like(xs_ref)


def gather_rows(h, order, nvalid, n_tiles, tg):
    D = h.shape[1]
    grid_spec = pltpu.PrefetchScalarGridSpec(
        num_scalar_prefetch=2,
        grid=(n_tiles,),
        in_specs=[pl.BlockSpec(memory_space=pl.ANY)],
        out_specs=pl.BlockSpec((tg, D), lambda i, order, nv: (i, 0)),
        scratch_shapes=[pltpu.VMEM((2, tg, D), F32), pltpu.SemaphoreType.DMA((2,))],
    )
    return pl.pallas_call(
        functools.partial(_gather_kernel, tg=tg),
        grid_spec=grid_spec,
        out_shape=jax.ShapeDtypeStruct((n_tiles * tg, D), BF16),
        compiler_params=_cparams(1, 40),
        name="moe_gather",
    )(order, nvalid, h)


def _gmm_kernel(te_ref, src_ref, first_ref, nxt_ref, lastrun_ref, quart_ref, nv_ref, *refs, nw, epi, lr, tn):
    x_ref = refs[0]
    w_hbm = refs[1:1 + nw]
    out_ref = refs[1 + nw]
    stage, wbf, sem = refs[2 + nw:5 + nw]
    n = pl.program_id(0)
    i = pl.program_id(1)

    def w_copy(j, e, nb):
        return pltpu.make_async_copy(w_hbm[j].at[lr, e, :, pl.ds(nb * tn, tn)], stage.at[j], sem.at[j])

    @pl.when((n == 0) & (i == 0))
    def _():
        for j in range(nw):
            w_copy(j, te_ref[0], 0).start()

    @pl.when(first_ref[i] == 1)
    def _():
        for j in range(nw):
            w_copy(j, te_ref[i], n).wait()
        _cast_blocks(stage, wbf, nw)
        nb_next = n + lastrun_ref[i]

        @pl.when(nb_next < pl.num_programs(0))
        def _():
            for j in range(nw):
                w_copy(j, nxt_ref[i], nb_next).start()

    def compute(rows):
        x = x_ref[:rows]
        accs = [jnp.dot(x, wbf[j], preferred_element_type=F32) for j in range(nw)]
        if epi == "swiglu":
            g, u = accs
            out = g * _sigmoid(g) * u
        else:
            out = accs[0]
        out_ref[:rows] = out.astype(out_ref.dtype)

    tg = x_ref.shape[0]
    valid = i < nv_ref[0]
    for nq in range(1, 5):
        @pl.when(valid & (quart_ref[i] == nq))
        def _(nq=nq):
            rows = nq * (tg // 4)
            compute(rows)
            if rows < tg:
                out_ref[rows:] = jnp.zeros((tg - rows, out_ref.shape[1]), out_ref.dtype)

    @pl.when(jnp.logical_not(valid))
    def _():
        out_ref[...] = jnp.zeros_like(out_ref)


def grouped_matmul(xs, ws, lr, tile_meta, *, epi, out_dtype, tg, tn, name):
    P, K = xs.shape
    N = ws[0].shape[-1]
    nw = len(ws)
    tn = min(tn, N)
    assert N % tn == 0 and P % tg == 0
    grid_spec = pltpu.PrefetchScalarGridSpec(
        num_scalar_prefetch=len(tile_meta),
        grid=(N // tn, P // tg),
        in_specs=[pl.BlockSpec((tg, K), lambda n, i, te, src, *_: (src[i], 0))]
        + [pl.BlockSpec(memory_space=pl.ANY)] * nw,
        out_specs=pl.BlockSpec((tg, tn), lambda n, i, *_: (i, n)),
        scratch_shapes=[pltpu.VMEM((nw, K, tn), F32), pltpu.VMEM((nw, K, tn), BF16),
                        pltpu.SemaphoreType.DMA((nw,))],
    )
    obytes = jnp.dtype(out_dtype).itemsize
    vmem = 2 * tg * K * 2 + nw * K * tn * 6 + 2 * tg * tn * obytes + (nw + 1) * tg * tn * 4
    return pl.pallas_call(
        functools.partial(_gmm_kernel, nw=nw, epi=epi, lr=lr, tn=tn),
        grid_spec=grid_spec,
        out_shape=jax.ShapeDtypeStruct((P, N), out_dtype),
        compiler_params=_cparams(2, vmem // MIB + VMEM_SLACK_MIB),
        name=name,
    )(*tile_meta, xs, *ws)


def _combine_kernel(dest_ref, x_ref, gate_ref, ys_hbm, out_ref, buf, sem, *, tc, top_k):
    i = pl.program_id(0)
    slot = i % 2

    def start_tile(t, s):
        def body(r, carry):
            for k in range(top_k):
                _row_copy(ys_hbm, buf.at[s, k], dest_ref[(t * tc + r) * top_k + k], r,
                          sem.at[s]).start(priority=k % 2)
            return carry
        lax.fori_loop(0, tc, body, 0, unroll=4)

    def wait_tile(s):
        for k in range(top_k):
            pltpu.make_async_copy(ys_hbm.at[pl.ds(0, tc)], buf.at[s, k], sem.at[s]).wait()

    @pl.when(i == 0)
    def _():
        start_tile(0, 0)

    @pl.when(i + 1 < pl.num_programs(0))
    def _():
        start_tile(i + 1, 1 - slot)

    wait_tile(slot)
    g = gate_ref[...]
    out = x_ref[...]
    for k in range(top_k):
        out = out + g[:, k:k + 1] * buf[slot, k]
    out_ref[...] = out


def moe_combine(x, gates, ys, dest_flat, top_k, tc=256):
    S, D = x.shape
    tc = min(tc, S)
    grid_spec = pltpu.PrefetchScalarGridSpec(
        num_scalar_prefetch=1,
        grid=(S // tc,),
        in_specs=[pl.BlockSpec((tc, D), lambda i, d: (i, 0)),
                  pl.BlockSpec((tc, LANES), lambda i, d: (i, 0)),
                  pl.BlockSpec(memory_space=pl.ANY)],
        out_specs=pl.BlockSpec((tc, D), lambda i, d: (i, 0)),
        scratch_shapes=[pltpu.VMEM((2, top_k, tc, D), F32), pltpu.SemaphoreType.DMA((2,))],
    )
    return pl.pallas_call(
        functools.partial(_combine_kernel, tc=tc, top_k=top_k),
        grid_spec=grid_spec,
        out_shape=jax.ShapeDtypeStruct((S, D), F32),
        compiler_params=_cparams(1, 48),
        name="moe_combine",
    )(dest_flat, x, gates, ys)


def moe_layer(x, norm_ffn3, l, w_router, b_router, w_e_gate, w_e_up, w_e_down, lr, cfg):
    S, D = x.shape
    E, top_k, tg = cfg.n_experts, 2, cfg.tg
    tg = min(tg, S)
    wr = jnp.pad(w_router, ((0, 0), (0, 0), (0, LANES - E)))
    br = jnp.pad(b_router, ((0, 0), (0, LANES - E)))[:, None, :]
    h, meta, gates, cnt = norm_router(x, norm_ffn3, wr, br, l, lr, E)

    eid, rank = meta[:, 0:top_k], meta[:, top_k:2 * top_k]
    counts = cnt[0, :E].astype(I32)
    padded = (counts + tg - 1) // tg * tg
    ends = jnp.cumsum(padded)
    dest = (ends - padded)[eid] + rank
    n_tiles = (S * top_k) // tg + E
    order = jnp.zeros((n_tiles * tg,), I32).at[dest.reshape(-1)].set(
        jnp.repeat(jnp.arange(S, dtype=I32), top_k))
    nvalid = (ends[-1] // tg).astype(I32)
    tile = jnp.arange(n_tiles, dtype=I32)
    src = jnp.minimum(tile, nvalid - 1)
    te = jnp.sum((src * tg)[:, None] >= ends[None, :], axis=1).astype(I32)
    first = ((tile == 0) | (te != jnp.roll(te, 1))).astype(I32)
    later = (tile[None, :] > tile[:, None]) & (te[None, :] != te[:, None])
    nxt_idx = jnp.min(jnp.where(later, tile[None, :], n_tiles), axis=1)
    lastrun = (nxt_idx == n_tiles).astype(I32)
    nxt = jnp.where(lastrun == 1, te[0], te[jnp.minimum(nxt_idx, n_tiles - 1)]).astype(I32)
    tile_rows = jnp.clip((counts + ends - padded)[te] - src * tg, 0, tg)
    quarters = ((tile_rows + tg // 4 - 1) // (tg // 4)).astype(I32)
    nv = nvalid.reshape(1)
    tile_meta = (te, src, first, nxt, lastrun, quarters, nv)

    xs = gather_rows(h, order, nv, n_tiles, tg)
    a = grouped_matmul(xs, [w_e_gate, w_e_up], lr, tile_meta, epi="swiglu", out_dtype=BF16,
                       tg=tg, tn=cfg.tn_b, name="moe_gate_up")
    ys = grouped_matmul(a, [w_e_down], lr, tile_meta, epi="plain", out_dtype=F32,
                        tg=tg, tn=cfg.tn_gd, name="moe_down")
    return moe_combine(x, gates, ys, dest.reshape(-1), top_k)


def _forward(x, mem, rel_bias, norm_mix, norm_ffn, norm_mem, w_in, qn_a, kn_a, conv_w, conv_b,
             gate_bias_b, hnorm_b, w_mem_kv, qn_m, kn_m, w_br_a, w_br_b, w_br_m, w_out,
             w_ff_gate, w_ff_up, w_ff_down, w_router, b_router, w_e_gate, w_e_up, w_e_down, cfg):
    B, S, D = x.shape
    assert B == 1 and mem.shape[0] == 1
    depth = norm_mix.shape[0]
    ng = len(cfg.a_groups)
    w_a = ng * cfg.heads_per_group * cfg.hd_a
    w_b = cfg.h_b * cfg.hd_b
    w_m = cfg.h_m * cfg.hd_m
    if_col0 = 3 * w_a + 3 * w_b
    tail_col0 = if_col0 + 2 * cfg.h_b
    d_ff = w_ff_gate.shape[-1]
    tk_down = min(cfg.tk_down, _round_up(d_ff, LANES))
    d_ff_p = _round_up(d_ff, tk_down)

    def row3(p):
        return p[:, None, :]

    x = x.reshape(S, D)
    mem2 = mem.reshape(mem.shape[1], D)
    norm_mix3, norm_ffn3, norm_mem3 = row3(norm_mix), row3(norm_ffn), row3(norm_mem)
    qn_a3, kn_a3, qn_m3, kn_m3 = row3(qn_a), row3(kn_a), row3(qn_m), row3(kn_m)
    conv_b3, hnorm3 = row3(conv_b), row3(hnorm_b)
    gbias3 = row3(jnp.pad(gate_bias_b, ((0, 0), (0, LANES - 2 * cfg.h_b))))
    mm = functools.partial(matmul, tm=cfg.tm)
    gw = cfg.heads_per_group * cfg.hd_a
    w_in_t = jnp.swapaxes(w_in, 1, 2)

    for l in range(depth):
        h = rmsnorm(x, norm_mix3, l)
        proj_g = [mm(h, [w_in_t], lead=l, trans=True, n_out=3 * gw, tn=gw, phases=cfg.a_groups[g][1],
                     blk_of=lambda n, g=g: n * ng + g, name=f"proj_a{g}").reshape(
                         cfg.a_groups[g][1], S // cfg.a_groups[g][1], 3 * gw)
                  for g in range(ng)]
        proj_b = mm(h, [w_in_t], lead=l, trans=True, col0=3 * w_a, n_out=3 * w_b, tn=cfg.tn_b,
                    name="proj_b")
        gates_b = mm(h, [w_in_t], lead=l, trans=True, col0=if_col0, n_out=LANES, out_dtype=F32,
                     tn=LANES, name="proj_if")
        proj_t = mm(h, [w_in_t], lead=l, trans=True, col0=tail_col0, n_out=w_b + w_m, tn=cfg.tn_b,
                    name="proj_tail")
        proj_gt = mm(h, [w_in_t], lead=l, trans=True, col0=tail_col0 + w_b + w_m, n_out=3 * D,
                     tn=cfg.tn, name="proj_gates")

        outs, lses = zip(*[band_attention(proj_g[g], rel_bias, qn_a3, kn_a3, l, g, cfg)
                           for g in range(ng)])
        y_a = alpha_merge(outs, lses)
        y_b = mlstm(proj_b, gates_b, proj_t, conv_w, conv_b3, gbias3, hnorm3, l, cfg)
        hm = rmsnorm(mem2, norm_mem3, l)
        kv = mm(hm, [w_mem_kv], lead=l, n_out=2 * w_m, tn=cfg.tn, name="mem_kv")
        y_m = cross_attention(proj_t, w_b, kv, qn_m3, kn_m3, l, cfg)
        y = gated_merge(y_a, y_b, y_m, w_br_a, w_br_b, w_br_m, proj_gt, 0, l, D,
                        tm=cfg.tm // 2, tn=cfg.tn)
        x = mm(y, [w_out], lead=l, n_out=D, epi="residual", res=x, out_dtype=F32, tn=cfg.tn_b,
               name="out_proj")

        if l % 2 == 0:
            ld = l // 2
            h2 = rmsnorm(x, norm_ffn3, l)
            a = matmul(h2, [w_ff_gate, w_ff_up], lead=ld, n_out=d_ff_p, epi="swiglu", n_valid=d_ff,
                       tm=cfg.tm // 2, tn=cfg.tn2, name="ffn_gate_up")
            wd = jnp.pad(w_ff_down[ld], ((0, d_ff_p - d_ff), (0, 0))).astype(BF16)
            x = matmul_ktiled_residual(a, wd, x, tm=cfg.tm, tn=cfg.tn_down, tk=tk_down, name="ffn_down")
        else:
            x = moe_layer(x, norm_ffn3, l, w_router, b_router, w_e_gate, w_e_up, w_e_down, l // 2, cfg)
    return x.reshape(B, S, D)


def kernel(x, mem, rel_bias, norm_mix, norm_ffn, norm_mem, w_in, qn_a, kn_a, conv_w, conv_b, gate_bias_b, hnorm_b, w_mem_kv, qn_m, kn_m, w_br_a, w_br_b, w_br_m, w_out, w_ff_gate, w_ff_up, w_ff_down, w_router, b_router, w_e_gate, w_e_up, w_e_down):
    return _forward(x, mem, rel_bias, norm_mix, norm_ffn, norm_mem, w_in, qn_a, kn_a, conv_w, conv_b,
                    gate_bias_b, hnorm_b, w_mem_kv, qn_m, kn_m, w_br_a, w_br_b, w_br_m, w_out,
                    w_ff_gate, w_ff_up, w_ff_down, w_router, b_router, w_e_gate, w_e_up, w_e_down,
                    Cfg())
```
